```python
import jax, jax.numpy as jnp
from jax import lax
import numpy as np

D_MODEL = 1024
BATCH = 8
SEQ = 4096
DEPTH = 1

MEM_LEN = 256
EPS = 1e-6
MIX_WIDTH = D_MODEL
RNN_WIDTH = MIX_WIDTH // 2
RNN_HEADS = 8
RNN_HEAD_DIM = RNN_WIDTH // RNN_HEADS
CONV_WIDTH = 4
LRU_C = 8.0
SB_WIDTH = MIX_WIDTH - RNN_WIDTH
SB_HEADS = 8
SB_HEAD_DIM = SB_WIDTH // SB_HEADS
Q_BLOCK = 128
PROJ_WIDTH = 2 * RNN_WIDTH + 3 * SB_WIDTH
XA_HEADS = 4
XA_HEAD_DIM = D_MODEL // XA_HEADS
N_GROUPS = 4
EXPERTS_PER_GROUP = 4
TOP_K = 2
D_EXPERT = D_MODEL // 4

kernel_name = "hymba_rglru_stickbreak_hmoe_block"


def rms_norm(x, gain):
    xf = x.astype(jnp.float32)
    var = jnp.mean(xf * xf, axis=-1, keepdims=True)
    return (xf * lax.rsqrt(var + EPS) * gain.astype(jnp.float32)).astype(x.dtype)


def causal_depthwise_conv(x, w, b):
    c = x.shape[-1]
    y = lax.conv_general_dilated(
        x, w[:, None, :].astype(x.dtype), window_strides=(1,),
        padding=[(CONV_WIDTH - 1, 0)], dimension_numbers=('NWC', 'WIO', 'NWC'),
        feature_group_count=c)
    return y + b.astype(x.dtype)


def rg_lru(x, w_a, b_a, w_x, b_x, lam):
    bsz, s, _ = x.shape
    xh = x.reshape(bsz, s, RNN_HEADS, RNN_HEAD_DIM)
    r = jax.nn.sigmoid(jnp.einsum('bshi,hij->bshj', xh, w_a) + b_a).reshape(bsz, s, RNN_WIDTH)
    i = jax.nn.sigmoid(jnp.einsum('bshi,hij->bshj', xh, w_x) + b_x).reshape(bsz, s, RNN_WIDTH)
    log_a = -LRU_C * r.astype(jnp.float32) * jax.nn.softplus(-lam.astype(jnp.float32))
    a = jnp.exp(log_a)
    b = jnp.sqrt(-jnp.expm1(2.0 * log_a)) * (i * x).astype(jnp.float32)

    def combine(left, right):
        a_l, b_l = left
        a_r, b_r = right
        return a_l * a_r, a_r * b_l + b_r

    _, h = lax.associative_scan(combine, (a, b), axis=1)
    return h.astype(x.dtype)


def stick_breaking_attention(q, k, v):
    bsz, s, nh, dh = q.shape
    n_blocks = s // Q_BLOCK
    scale = dh ** -0.5
    q_blocks = q.reshape(bsz, n_blocks, Q_BLOCK, nh, dh).transpose(1, 0, 3, 2, 4)
    key_pos = jnp.arange(s)

    def one_block(args):
        q_blk, blk = args
        query_pos = blk * Q_BLOCK + jnp.arange(Q_BLOCK)
        z = jnp.einsum('bhqd,bkhd->bhqk', q_blk, k).astype(jnp.float32) * scale
        mask = key_pos[None, :] < query_pos[:, None]
        log_not = jnp.where(mask, jax.nn.log_sigmoid(-z), 0.0)
        later = lax.cumsum(log_not, axis=3, reverse=True) - log_not
        w = jnp.where(mask, jnp.exp(jax.nn.log_sigmoid(z) + later), 0.0)
        return jnp.einsum('bhqk,bkhd->bqhd', w.astype(v.dtype), v)

    out = lax.map(one_block, (q_blocks, jnp.arange(n_blocks)))
    return out.transpose(1, 0, 2, 3, 4).reshape(bsz, s, nh * dh)


def hybrid_mixer(h, w_in, conv_w, conv_b, w_a, b_a, w_x, b_x, lam, g_rnn, g_sb, w_out):
    bsz, s, _ = h.shape
    proj = h @ w_in
    x_rnn, gate_rnn, q, k, v = jnp.split(
        proj, [RNN_WIDTH, 2 * RNN_WIDTH, 2 * RNN_WIDTH + SB_WIDTH, 2 * RNN_WIDTH + 2 * SB_WIDTH], axis=-1)
    x_rnn = causal_depthwise_conv(x_rnn, conv_w, conv_b)
    y_rnn = rg_lru(x_rnn, w_a, b_a, w_x, b_x, lam) * jax.nn.gelu(gate_rnn)
    hd = (bsz, s, SB_HEADS, SB_HEAD_DIM)
    y_sb = stick_breaking_attention(q.reshape(hd), k.reshape(hd), v.reshape(hd))
    y = jnp.concatenate([rms_norm(y_rnn, g_rnn), rms_norm(y_sb, g_sb)], axis=-1)
    return y @ w_out


def memory_cross_attention(h, m, w_q, w_k, w_v, w_o):
    bsz, s, d = h.shape
    n_mem = m.shape[1]
    q = (h @ w_q).reshape(bsz, s, XA_HEADS, XA_HEAD_DIM)
    k = (m @ w_k).reshape(bsz, n_mem, XA_HEADS, XA_HEAD_DIM)
    v = (m @ w_v).reshape(bsz, n_mem, XA_HEADS, XA_HEAD_DIM)
    scores = jnp.einsum('bshd,bmhd->bhsm', q, k).astype(jnp.float32) * (XA_HEAD_DIM ** -0.5)
    p = jax.nn.softmax(scores, axis=-1).astype(v.dtype)
    o = jnp.einsum('bhsm,bmhd->bshd', p, v).reshape(bsz, s, d)
    return o @ w_o


def hierarchical_moe(h, w_group_router, b_group_router, w_expert_router, b_expert_router,
                     w_gate, w_up, w_down):
    bsz, s, d = h.shape
    t = h.reshape(-1, d)
    group_prob = jax.nn.softmax((t @ w_group_router).astype(jnp.float32) + b_group_router, axis=-1)
    group_p, group_idx = lax.top_k(group_prob, 1)
    expert_logits = jnp.einsum('nd,gde->nge', t, w_expert_router).astype(jnp.float32) + b_expert_router
    chosen = jnp.take_along_axis(expert_logits, group_idx[:, :, None], axis=1)[:, 0]
    top_logit, top_idx = lax.top_k(chosen, TOP_K)
    top_w = jax.nn.softmax(top_logit, axis=-1) * group_p
    expert_w = jnp.sum(jax.nn.one_hot(top_idx, EXPERTS_PER_GROUP, dtype=jnp.float32) * top_w[..., None], axis=1)
    combine = jax.nn.one_hot(group_idx[:, 0], N_GROUPS, dtype=jnp.float32)[:, :, None] * expert_w[:, None, :]
    out = jnp.zeros_like(t)
    for g in range(N_GROUPS):
        hid = jax.nn.silu(jnp.einsum('nd,edf->nef', t, w_gate[g])) * jnp.einsum('nd,edf->nef', t, w_up[g])
        hid = hid * combine[:, g, :, None].astype(hid.dtype)
        out = out + jnp.einsum('nef,efd->nd', hid, w_down[g])
    return out.reshape(bsz, s, d)


def setup_inputs(seed: int = 0) -> dict:
    key = jax.random.key(seed)
    ks = jax.random.split(key, 32)
    f32 = jnp.float32
    L = DEPTH

    def nrm(k, shape, fan_in):
        return jax.random.normal(k, shape, f32) * (fan_in ** -0.5)

    def gain(k, shape):
        return 1.0 + 0.02 * jax.random.normal(k, shape, f32)

    u = jax.random.uniform(ks[9], (L, RNN_WIDTH), f32, 0.9, 0.999)
    a0 = u ** (1.0 / LRU_C)
    lam = jnp.log(a0) - jnp.log1p(-a0)
    return {
        "x": jax.random.normal(ks[0], (BATCH, SEQ, D_MODEL), f32),
        "mem": jax.random.normal(ks[1], (BATCH, MEM_LEN, D_MODEL), f32),
        "norm_mix": gain(ks[2], (L, D_MODEL)),
        "w_in": nrm(ks[3], (L, D_MODEL, PROJ_WIDTH), D_MODEL),
        "conv_w": nrm(ks[4], (L, CONV_WIDTH, RNN_WIDTH), CONV_WIDTH),
        "conv_b": 0.01 * jax.random.normal(ks[5], (L, RNN_WIDTH), f32),
        "lru_w_a": nrm(ks[6], (L, RNN_HEADS, RNN_HEAD_DIM, RNN_HEAD_DIM), RNN_HEAD_DIM),
        "lru_b_a": 0.01 * jax.random.normal(ks[7], (L, RNN_HEADS, RNN_HEAD_DIM), f32),
        "lru_w_x": nrm(ks[8], (L, RNN_HEADS, RNN_HEAD_DIM, RNN_HEAD_DIM), RNN_HEAD_DIM),
        "lru_b_x": 0.01 * jax.random.normal(ks[10], (L, RNN_HEADS, RNN_HEAD_DIM), f32),
        "lru_lambda": lam,
        "norm_rnn_out": gain(ks[11], (L, RNN_WIDTH)),
        "norm_sb_out": gain(ks[12], (L, SB_WIDTH)),
        "w_out": nrm(ks[13], (L, MIX_WIDTH, D_MODEL), MIX_WIDTH),
        "norm_xattn": gain(ks[14], (L, D_MODEL)),
        "norm_mem": gain(ks[15], (L, D_MODEL)),
        "xa_w_q": nrm(ks[16], (L, D_MODEL, D_MODEL), D_MODEL),
        "xa_w_k": nrm(ks[17], (L, D_MODEL, D_MODEL), D_MODEL),
        "xa_w_v": nrm(ks[18], (L, D_MODEL, D_MODEL), D_MODEL),
        "xa_w_o": nrm(ks[19], (L, D_MODEL, D_MODEL), D_MODEL),
        "norm_moe": gain(ks[20], (L, D_MODEL)),
        "w_group_router": nrm(ks[21], (L, D_MODEL, N_GROUPS), D_MODEL),
        "b_group_router": 0.01 * jax.random.normal(ks[22], (L, N_GROUPS), f32),
        "w_expert_router": nrm(ks[23], (L, N_GROUPS, D_MODEL, EXPERTS_PER_GROUP), D_MODEL),
        "b_expert_router": 0.01 * jax.random.normal(ks[24], (L, N_GROUPS, EXPERTS_PER_GROUP), f32),
        "w_gate": nrm(ks[25], (L, N_GROUPS, EXPERTS_PER_GROUP, D_MODEL, D_EXPERT), D_MODEL),
        "w_up": nrm(ks[26], (L, N_GROUPS, EXPERTS_PER_GROUP, D_MODEL, D_EXPERT), D_MODEL),
        "w_down": nrm(ks[27], (L, N_GROUPS, EXPERTS_PER_GROUP, D_EXPERT, D_MODEL), D_EXPERT),
        "norm_final": gain(ks[28], (D_MODEL,)),
    }


def reference(x, mem, norm_mix, w_in, conv_w, conv_b, lru_w_a, lru_b_a, lru_w_x, lru_b_x,
              lru_lambda, norm_rnn_out, norm_sb_out, w_out, norm_xattn, norm_mem,
              xa_w_q, xa_w_k, xa_w_v, xa_w_o, norm_moe, w_group_router, b_group_router,
              w_expert_router, b_expert_router, w_gate, w_up, w_down, norm_final):
    for l in range(DEPTH):
        x = x + hybrid_mixer(rms_norm(x, norm_mix[l]), w_in[l], conv_w[l], conv_b[l],
                             lru_w_a[l], lru_b_a[l], lru_w_x[l], lru_b_x[l], lru_lambda[l],
                             norm_rnn_out[l], norm_sb_out[l], w_out[l])
        x = x + memory_cross_attention(rms_norm(x, norm_xattn[l]), rms_norm(mem, norm_mem[l]),
                                       xa_w_q[l], xa_w_k[l], xa_w_v[l], xa_w_o[l])
        x = x + hierarchical_moe(rms_norm(x, norm_moe[l]), w_group_router[l], b_group_router[l],
                                 w_expert_router[l], b_expert_router[l],
                                 w_gate[l], w_up[l], w_down[l])
    return rms_norm(x, norm_final)
```

```python
import functools
import math

import jax
import jax.numpy as jnp
from jax import lax
from jax.experimental import pallas as pl
from jax.experimental.pallas import tpu as pltpu

F32 = jnp.float32
BF16 = jnp.bfloat16

D_MODEL = 1024
RNN_WIDTH = 512
RNN_HEADS = 8
RNN_HEAD_DIM = 64
CONV_WIDTH = 4
LRU_C = 8.0
SB_WIDTH = 512
SB_HEADS = 8
SB_HEAD_DIM = 64
PROJ_WIDTH = 2 * RNN_WIDTH + 3 * SB_WIDTH
XA_HEADS = 4
XA_HEAD_DIM = D_MODEL // XA_HEADS
N_GROUPS = 4
EXPERTS_PER_GROUP = 4
D_EXPERT = D_MODEL // 4
EPS = 1e-6

SUBLANES = 8
VMEM_LIMIT = 56 * 1024 * 1024

LOG2E = 1.4426950408889634
Q_SCALE = (SB_HEAD_DIM ** -0.5) * LOG2E

MIX_TILE = 512
ATT_TQ = 256
ATT_TK = 256
XA_TILE = 512
MOE_TILE = 1024


def _rms(x, gain):
    var = jnp.mean(x * x, axis=-1, keepdims=True)
    return x * lax.rsqrt(var + EPS) * gain


def _dot(a, b):
    return jnp.dot(a, b, preferred_element_type=F32)


def _dot_nt(a, b):
    return lax.dot_general(a, b, (((1,), (1,)), ((), ())), preferred_element_type=F32)


def _mixer_in_kernel(x_ref, g_ref, w_ref, cw_ref, cb_ref, wg_ref, bg_ref, lam_ref, grnn_ref,
                     yr_ref, q_ref, k_ref, v_ref,
                     xpad, a_s, b_s, hcar):
    t = pl.program_id(1)
    tile = x_ref.shape[0]

    @pl.when(t == 0)
    def _():
        xpad[0:SUBLANES, :] = jnp.zeros((SUBLANES, RNN_WIDTH), F32)
        hcar[...] = jnp.zeros_like(hcar)

    h = _rms(x_ref[...], g_ref[...]).astype(BF16)
    pr = _dot(h, w_ref[:, 0:2 * RNN_WIDTH])
    qkv = _dot(h, w_ref[:, 2 * RNN_WIDTH:PROJ_WIDTH])
    q_ref[...] = (qkv[:, 0:SB_WIDTH] * Q_SCALE).astype(BF16)
    k_ref[...] = qkv[:, SB_WIDTH:2 * SB_WIDTH].astype(BF16)
    v_ref[...] = qkv[:, 2 * SB_WIDTH:3 * SB_WIDTH].astype(BF16)

    xpad[SUBLANES:SUBLANES + tile, :] = pr[:, 0:RNN_WIDTH]
    xc = cb_ref[...] + cw_ref[0:1, :] * xpad[SUBLANES - 3:SUBLANES - 3 + tile, :]
    for kk in range(1, CONV_WIDTH):
        off = SUBLANES - (CONV_WIDTH - 1) + kk
        xc = xc + cw_ref[kk:kk + 1, :] * xpad[off:off + tile, :]
    xpad[0:SUBLANES, :] = xpad[tile:tile + SUBLANES, :]

    xcb = xc.astype(BF16)
    half = RNN_WIDTH // 2
    g0 = _dot(xcb[:, 0:half], wg_ref[0])
    g1 = _dot(xcb[:, half:RNN_WIDTH], wg_ref[1])
    r_pre = jnp.concatenate([g0[:, 0:half], g1[:, 0:half]], axis=-1) + bg_ref[0:1, :]
    i_pre = jnp.concatenate([g0[:, half:2 * half], g1[:, half:2 * half]], axis=-1) + bg_ref[1:2, :]
    r = jax.nn.sigmoid(r_pre)
    i = jax.nn.sigmoid(i_pre)
    lam = lam_ref[...]
    neg_lam = -lam
    softplus_neg_lam = jnp.maximum(neg_lam, 0.0) + jnp.log(1.0 + jnp.exp(-jnp.abs(lam)))
    log_a = (-LRU_C) * r * softplus_neg_lam
    a = jnp.exp(log_a)
    b = jnp.sqrt(1.0 - a * a) * (i * xc)
    a_s[...] = a
    b_s[...] = b

    row = lax.broadcasted_iota(jnp.int32, (SUBLANES, RNN_WIDTH), 0)

    def scan_body(gidx, hc):
        r0 = pl.multiple_of(gidx * SUBLANES, SUBLANES)
        ag = a_s[pl.ds(r0, SUBLANES), :]
        bgr = b_s[pl.ds(r0, SUBLANES), :]
        for d in (1, 2, 4):
            a_sh = jnp.where(row >= d, pltpu.roll(ag, d, 0), 1.0)
            b_sh = jnp.where(row >= d, pltpu.roll(bgr, d, 0), 0.0)
            bgr = ag * b_sh + bgr
            ag = ag * a_sh
        hg = ag * hc + bgr
        b_s[pl.ds(r0, SUBLANES), :] = hg
        return hg[SUBLANES - 1:SUBLANES, :]

    hlast = lax.fori_loop(0, tile // SUBLANES, scan_body, hcar[...], unroll=4)
    hcar[...] = hlast

    gate = pr[:, RNN_WIDTH:2 * RNN_WIDTH]
    gelu = 0.5 * gate * (1.0 + jnp.tanh(math.sqrt(2.0 / math.pi) * (gate + 0.044715 * (gate * gate * gate))))
    y = b_s[...] * gelu
    yr_ref[...] = _rms(y, grnn_ref[...]).astype(BF16)


def _mixer_in(x, g, w_in, conv_w, conv_b, w_gates, b_gates, lam, g_rnn):
    bsz, seq, _ = x.shape
    tile = MIX_TILE
    grid = (bsz, seq // tile)
    const2 = lambda b, t: (0, 0)
    tok_spec = lambda width: pl.BlockSpec((None, tile, width), lambda b, t: (b, t, 0))
    out_sds = jax.ShapeDtypeStruct((bsz, seq, SB_WIDTH), BF16)
    return pl.pallas_call(
        _mixer_in_kernel,
        out_shape=(out_sds, out_sds, out_sds, out_sds),
        grid=grid,
        in_specs=[
            tok_spec(D_MODEL),
            pl.BlockSpec((1, D_MODEL), const2),
            pl.BlockSpec((D_MODEL, PROJ_WIDTH), const2),
            pl.BlockSpec((CONV_WIDTH, RNN_WIDTH), const2),
            pl.BlockSpec((1, RNN_WIDTH), const2),
            pl.BlockSpec((2, RNN_WIDTH // 2, RNN_WIDTH), lambda b, t: (0, 0, 0)),
            pl.BlockSpec((2, RNN_WIDTH), const2),
            pl.BlockSpec((1, RNN_WIDTH), const2),
            pl.BlockSpec((1, RNN_WIDTH), const2),
        ],
        out_specs=(tok_spec(RNN_WIDTH), tok_spec(SB_WIDTH), tok_spec(SB_WIDTH), tok_spec(SB_WIDTH)),
        scratch_shapes=[
            pltpu.VMEM((tile + SUBLANES, RNN_WIDTH), F32),
            pltpu.VMEM((tile, RNN_WIDTH), F32),
            pltpu.VMEM((tile, RNN_WIDTH), F32),
            pltpu.VMEM((1, RNN_WIDTH), F32),
        ],
        compiler_params=pltpu.CompilerParams(
            dimension_semantics=("arbitrary", "arbitrary"), vmem_limit_bytes=VMEM_LIMIT),
        name="mixer_in",
    )(x, g, w_in, conv_w, conv_b, w_gates, b_gates, lam, g_rnn)


def _softplus2(z2):
    return jnp.maximum(z2, 0.0) + jnp.log2(1.0 + jnp.exp2(-jnp.abs(z2)))


def _suffix_sum(sp, tri):
    hi = sp.astype(BF16)
    lo = (sp - hi.astype(F32)).astype(BF16)
    return _dot(hi, tri) + _dot(lo, tri)


def _sb_attn_kernel(q_ref, k_ref, v_ref, o_ref, q_s, k_s, v_s, acc_s, car_s):
    seq = q_ref.shape[0]
    heads_per_step = q_ref.shape[1] // SB_HEAD_DIM
    n_q = seq // ATT_TQ
    for hh in range(heads_per_step):
        sl = slice(hh * SB_HEAD_DIM, (hh + 1) * SB_HEAD_DIM)
        q_s[hh] = q_ref[:, sl]
        k_s[hh] = k_ref[:, sl]
        v_s[hh] = v_ref[:, sl]

    krow = lax.broadcasted_iota(jnp.int32, (ATT_TK, ATT_TK), 0)
    kcol = lax.broadcasted_iota(jnp.int32, (ATT_TK, ATT_TK), 1)
    tri = jnp.where(krow >= kcol, 1.0, 0.0).astype(BF16)
    qrow = lax.broadcasted_iota(jnp.int32, (ATT_TQ, ATT_TK), 0)
    qcol = lax.broadcasted_iota(jnp.int32, (ATT_TQ, ATT_TK), 1)
    causal = qcol < qrow

    for hh in range(heads_per_step):
        sl = slice(hh * SB_HEAD_DIM, (hh + 1) * SB_HEAD_DIM)

        def q_body(qi, _):
            q0 = pl.multiple_of(qi * ATT_TQ, ATT_TQ)
            q = q_s[hh, pl.ds(q0, ATT_TQ), :]
            z = _dot_nt(q, k_s[hh, pl.ds(q0, ATT_TK), :])
            sp = jnp.where(causal, _softplus2(z), 0.0)
            cum = _suffix_sum(sp, tri)
            w = jnp.where(causal, jnp.exp2(z - cum), 0.0)
            acc_s[...] = _dot(w.astype(BF16), v_s[hh, pl.ds(q0, ATT_TK), :])
            car_s[...] = cum[:, 0:1]

            def kv_body(jj, _):
                k0 = pl.multiple_of((qi - 1 - jj) * ATT_TK, ATT_TK)
                zz = _dot_nt(q, k_s[hh, pl.ds(k0, ATT_TK), :])
                cc = _suffix_sum(_softplus2(zz), tri)
                carry = car_s[...]
                ww = jnp.exp2(zz - cc - carry)
                acc_s[...] += _dot(ww.astype(BF16), v_s[hh, pl.ds(k0, ATT_TK), :])
                car_s[...] = carry + cc[:, 0:1]
                return 0

            lax.fori_loop(0, qi, kv_body, 0)
            o_ref[pl.ds(q0, ATT_TQ), sl] = acc_s[...]
            return 0

        lax.fori_loop(0, n_q, q_body, 0)


def _sb_attn(q, k, v):
    bsz, seq, width = q.shape
    lanes = 128
    hps = lanes // SB_HEAD_DIM
    spec = pl.BlockSpec((None, seq, lanes), lambda b, p: (b, 0, p))
    return pl.pallas_call(
        _sb_attn_kernel,
        out_shape=jax.ShapeDtypeStruct((bsz, seq, width), F32),
        grid=(bsz, width // lanes),
        in_specs=[spec, spec, spec],
        out_specs=spec,
        scratch_shapes=[
            pltpu.VMEM((hps, seq, SB_HEAD_DIM), BF16),
            pltpu.VMEM((hps, seq, SB_HEAD_DIM), BF16),
            pltpu.VMEM((hps, seq, SB_HEAD_DIM), BF16),
            pltpu.VMEM((ATT_TQ, SB_HEAD_DIM), F32),
            pltpu.VMEM((ATT_TQ, 1), F32),
        ],
        compiler_params=pltpu.CompilerParams(
            dimension_semantics=("arbitrary", "arbitrary"), vmem_limit_bytes=VMEM_LIMIT),
        name="sb_attn",
    )(q, k, v)


def _mem_kv_kernel(m_ref, g_ref, wk_ref, wv_ref, k_ref, v_ref):
    hm = _rms(m_ref[...], g_ref[...]).astype(BF16)
    k_ref[...] = _dot(hm, wk_ref[...]).astype(BF16)
    v_ref[...] = _dot(hm, wv_ref[...]).astype(BF16)


def _mem_kv(mem, g, w_k, w_v):
    bsz, n_mem, _ = mem.shape
    const2 = lambda b: (0, 0)
    spec = pl.BlockSpec((None, n_mem, D_MODEL), lambda b: (b, 0, 0))
    sds = jax.ShapeDtypeStruct((bsz, n_mem, D_MODEL), BF16)
    return pl.pallas_call(
        _mem_kv_kernel,
        out_shape=(sds, sds),
        grid=(bsz,),
        in_specs=[spec, pl.BlockSpec((1, D_MODEL), const2),
                  pl.BlockSpec((D_MODEL, D_MODEL), const2), pl.BlockSpec((D_MODEL, D_MODEL), const2)],
        out_specs=(spec, spec),
        compiler_params=pltpu.CompilerParams(
            dimension_semantics=("arbitrary",), vmem_limit_bytes=VMEM_LIMIT),
        name="mem_kv",
    )(mem, g, w_k, w_v)


def _mix_xattn_kernel(x_ref, yr_ref, ysb_ref, gsb_ref, wout_ref, gxa_ref, wq_ref, km_ref, vm_ref, wo_ref,
                      o_ref):
    ysb = _rms(ysb_ref[...], gsb_ref[...]).astype(BF16)
    y = jnp.concatenate([yr_ref[...], ysb], axis=-1)
    x1 = x_ref[...] + _dot(y, wout_ref[...])
    hq = _rms(x1, gxa_ref[...]).astype(BF16)
    q = _dot(hq, wq_ref[...]).astype(BF16)
    scale = XA_HEAD_DIM ** -0.5
    outs = []
    for hd in range(XA_HEADS):
        sl = slice(hd * XA_HEAD_DIM, (hd + 1) * XA_HEAD_DIM)
        s = _dot_nt(q[:, sl], km_ref[:, sl]) * scale
        m = jnp.max(s, axis=-1, keepdims=True)
        p = jnp.exp(s - m)
        den = jnp.sum(p, axis=-1, keepdims=True)
        p = (p / den).astype(BF16)
        outs.append(_dot(p, vm_ref[:, sl]).astype(BF16))
    o = jnp.concatenate(outs, axis=-1)
    o_ref[...] = x1 + _dot(o, wo_ref[...])


def _mix_xattn(x, yr, ysb, g_sb, w_out, g_xa, w_q, kmem, vmem, w_o):
    bsz, seq, _ = x.shape
    n_mem = kmem.shape[1]
    tile = XA_TILE
    const2 = lambda b, t: (0, 0)
    tok_spec = lambda width: pl.BlockSpec((None, tile, width), lambda b, t: (b, t, 0))
    mem_spec = pl.BlockSpec((None, n_mem, D_MODEL), lambda b, t: (b, 0, 0))
    w_spec = pl.BlockSpec((D_MODEL, D_MODEL), const2)
    return pl.pallas_call(
        _mix_xattn_kernel,
        out_shape=jax.ShapeDtypeStruct((bsz, seq, D_MODEL), F32),
        grid=(bsz, seq // tile),
        in_specs=[tok_spec(D_MODEL), tok_spec(RNN_WIDTH), tok_spec(SB_WIDTH),
                  pl.BlockSpec((1, SB_WIDTH), const2), w_spec,
                  pl.BlockSpec((1, D_MODEL), const2), w_spec, mem_spec, mem_spec, w_spec],
        out_specs=tok_spec(D_MODEL),
        compiler_params=pltpu.CompilerParams(
            dimension_semantics=("arbitrary", "arbitrary"), vmem_limit_bytes=VMEM_LIMIT),
        name="mix_xattn",
    )(x, yr, ysb, g_sb, w_out, g_xa, w_q, kmem, vmem, w_o)


N_ROUTE = N_GROUPS + N_GROUPS * EXPERTS_PER_GROUP
ROUTE_PAD = 128


def _first_argmax(vals, lane, width):
    m = jnp.max(vals, axis=-1, keepdims=True)
    idx = jnp.min(jnp.where(vals == m, lane, width), axis=-1, keepdims=True)
    return m, idx


def _moe_final_kernel(x_ref, gmoe_ref, wr_ref, br_ref, wg_ref, wu_ref, wd_ref, gfin_ref,
                      o_ref, h_s, comb_s, acc_s):
    g = pl.program_id(1)

    @pl.when(g == 0)
    def _():
        ht = _rms(x_ref[...], gmoe_ref[...])
        hb = ht.astype(BF16)
        h_s[...] = hb
        h_lo = (ht - hb.astype(F32)).astype(BF16)
        logits = _dot(hb, wr_ref[0]) + _dot(h_lo, wr_ref[0]) + _dot(hb, wr_ref[1])
        logits = logits + br_ref[...]
        lane = lax.broadcasted_iota(jnp.int32, logits.shape, 1)
        neg = -jnp.inf
        gl = jnp.where(lane < N_GROUPS, logits, neg)
        gmax, gidx = _first_argmax(gl, lane, ROUTE_PAD)
        gden = jnp.sum(jnp.exp(gl - gmax), axis=-1, keepdims=True)
        group_p = 1.0 / gden
        e_lo = N_GROUPS + gidx * EXPERTS_PER_GROUP
        in_group = (lane >= e_lo) & (lane < e_lo + EXPERTS_PER_GROUP)
        el = jnp.where(in_group, logits, neg)
        m1, i1 = _first_argmax(el, lane, ROUTE_PAD)
        el2 = jnp.where(lane == i1, neg, el)
        m2, i2 = _first_argmax(el2, lane, ROUTE_PAD)
        e2 = jnp.exp(m2 - m1)
        w1 = group_p / (1.0 + e2)
        w2 = w1 * e2
        comb_s[...] = jnp.where(lane == i1, w1, 0.0) + jnp.where(lane == i2, w2, 0.0)
        acc_s[...] = jnp.zeros_like(acc_s)

    hb = h_s[...]
    lane = lax.broadcasted_iota(jnp.int32, comb_s.shape, 1)
    comb = comb_s[...]
    for e in range(EXPERTS_PER_GROUP):
        col = N_GROUPS + g * EXPERTS_PER_GROUP + e
        cw = jnp.sum(jnp.where(lane == col, comb, 0.0), axis=-1, keepdims=True)
        hg = _dot(hb, wg_ref[e])
        hu = _dot(hb, wu_ref[e])
        hid = (hg * jax.nn.sigmoid(hg)) * hu * cw
        acc_s[...] += _dot(hid.astype(BF16), wd_ref[e])

    @pl.when(g == N_GROUPS - 1)
    def _():
        o_ref[...] = _rms(x_ref[...] + acc_s[...], gfin_ref[...])


def _moe_final(x, g_moe, w_router, b_router, w_gate, w_up, w_down, g_final):
    n_tok = x.shape[0]
    tile = MOE_TILE
    const2 = lambda i, g: (0, 0)
    tok_spec = pl.BlockSpec((tile, D_MODEL), lambda i, g: (i, 0))
    return pl.pallas_call(
        _moe_final_kernel,
        out_shape=jax.ShapeDtypeStruct((n_tok, D_MODEL), F32),
        grid=(n_tok // tile, N_GROUPS),
        in_specs=[
            tok_spec,
            pl.BlockSpec((1, D_MODEL), const2),
            pl.BlockSpec((2, D_MODEL, ROUTE_PAD), lambda i, g: (0, 0, 0)),
            pl.BlockSpec((1, ROUTE_PAD), const2),
            pl.BlockSpec((EXPERTS_PER_GROUP, D_MODEL, D_EXPERT), lambda i, g: (g, 0, 0)),
            pl.BlockSpec((EXPERTS_PER_GROUP, D_MODEL, D_EXPERT), lambda i, g: (g, 0, 0)),
            pl.BlockSpec((EXPERTS_PER_GROUP, D_EXPERT, D_MODEL), lambda i, g: (g, 0, 0)),
            pl.BlockSpec((1, D_MODEL), const2),
        ],
        out_specs=tok_spec,
        scratch_shapes=[
            pltpu.VMEM((tile, D_MODEL), BF16),
            pltpu.VMEM((tile, ROUTE_PAD), F32),
            pltpu.VMEM((tile, D_MODEL), F32),
        ],
        compiler_params=pltpu.CompilerParams(
            dimension_semantics=("arbitrary", "arbitrary"), vmem_limit_bytes=VMEM_LIMIT),
        name="moe_final",
    )(x, g_moe, w_router, b_router, w_gate, w_up, w_down, g_final)


def _block_diag_gates(w_a, w_x):
    half_heads = RNN_HEADS // 2
    half = RNN_WIDTH // 2
    out = jnp.zeros((2, half, 2 * half), F32)
    for c in range(2):
        for hh in range(half_heads):
            r0 = hh * RNN_HEAD_DIM
            out = out.at[c, r0:r0 + RNN_HEAD_DIM, r0:r0 + RNN_HEAD_DIM].set(w_a[c * half_heads + hh])
            out = out.at[c, r0:r0 + RNN_HEAD_DIM, half + r0:half + r0 + RNN_HEAD_DIM].set(w_x[c * half_heads + hh])
    return out.astype(BF16)


def _router_weights(w_group, b_group, w_expert, b_expert):
    w = jnp.concatenate(
        [w_group, jnp.transpose(w_expert, (1, 0, 2)).reshape(D_MODEL, N_GROUPS * EXPERTS_PER_GROUP)], axis=-1)
    w = jnp.pad(w, ((0, 0), (0, ROUTE_PAD - N_ROUTE)))
    hi = w.astype(BF16)
    lo = (w - hi.astype(F32)).astype(BF16)
    b = jnp.concatenate([b_group, b_expert.reshape(-1)])
    b = jnp.pad(b, (0, ROUTE_PAD - N_ROUTE)).reshape(1, ROUTE_PAD)
    return jnp.stack([hi, lo]), b


def _layer(x, mem, p):
    bsz, seq, _ = x.shape
    row = lambda a: a.reshape(1, -1)
    w_gates = _block_diag_gates(p["lru_w_a"], p["lru_w_x"])
    b_gates = jnp.stack([p["lru_b_a"].reshape(-1), p["lru_b_x"].reshape(-1)])
    yr, q, k, v = _mixer_in(x, row(p["norm_mix"]), p["w_in"].astype(BF16), p["conv_w"], row(p["conv_b"]),
                            w_gates, b_gates, row(p["lru_lambda"]), row(p["norm_rnn_out"]))
    ysb = _sb_attn(q, k, v)
    kmem, vmem = _mem_kv(mem, row(p["norm_mem"]), p["xa_w_k"].astype(BF16), p["xa_w_v"].astype(BF16))
    x2 = _mix_xattn(x, yr, ysb, row(p["norm_sb_out"]), p["w_out"].astype(BF16), row(p["norm_xattn"]),
                    p["xa_w_q"].astype(BF16), kmem, vmem, p["xa_w_o"].astype(BF16))
    w_router, b_router = _router_weights(p["w_group_router"], p["b_group_router"],
                                         p["w_expert_router"], p["b_expert_router"])
    ne = N_GROUPS * EXPERTS_PER_GROUP
    out = _moe_final(x2.reshape(bsz * seq, D_MODEL), row(p["norm_moe"]), w_router, b_router,
                     p["w_gate"].reshape(ne, D_MODEL, D_EXPERT).astype(BF16),
                     p["w_up"].reshape(ne, D_MODEL, D_EXPERT).astype(BF16),
                     p["w_down"].reshape(ne, D_EXPERT, D_MODEL).astype(BF16),
                     row(p["norm_final"]))
    return out.reshape(bsz, seq, D_MODEL)


def kernel(x, mem, norm_mix, w_in, conv_w, conv_b, lru_w_a, lru_b_a, lru_w_x, lru_b_x, lru_lambda, norm_rnn_out, norm_sb_out, w_out, norm_xattn, norm_mem, xa_w_q, xa_w_k, xa_w_v, xa_w_o, norm_moe, w_group_router, b_group_router, w_expert_router, b_expert_router, w_gate, w_up, w_down, norm_final):
    depth = norm_mix.shape[0]
    assert depth == 1, "the fused final RMSNorm assumes a single layer"
    l = 0
    params = dict(
        norm_mix=norm_mix[l], w_in=w_in[l], conv_w=conv_w[l], conv_b=conv_b[l],
        lru_w_a=lru_w_a[l], lru_b_a=lru_b_a[l], lru_w_x=lru_w_x[l], lru_b_x=lru_b_x[l],
        lru_lambda=lru_lambda[l], norm_rnn_out=norm_rnn_out[l], norm_sb_out=norm_sb_out[l],
        w_out=w_out[l], norm_xattn=norm_xattn[l], norm_mem=norm_mem[l],
        xa_w_q=xa_w_q[l], xa_w_k=xa_w_k[l], xa_w_v=xa_w_v[l], xa_w_o=xa_w_o[l],
        norm_moe=norm_moe[l], w_group_router=w_group_router[l], b_group_router=b_group_router[l],
        w_expert_router=w_expert_router[l], b_expert_router=b_expert_router[l],
        w_gate=w_gate[l], w_up=w_up[l], w_down=w_down[l], norm_final=norm_final)
    return _layer(x, mem, params)
```

```python
import functools
import math

import jax
import jax.numpy as jnp
from jax import lax
from jax.experimental import pallas as pl
from jax.experimental.pallas import tpu as pltpu

F32 = jnp.float32
BF16 = jnp.bfloat16

D_MODEL = 1024
RNN_WIDTH = 512
RNN_HEADS = 8
RNN_HEAD_DIM = 64
CONV_WIDTH = 4
LRU_C = 8.0
SB_WIDTH = 512
SB_HEADS = 8
SB_HEAD_DIM = 64
PROJ_WIDTH = 2 * RNN_WIDTH + 3 * SB_WIDTH
XA_HEADS = 4
XA_HEAD_DIM = D_MODEL // XA_HEADS
N_GROUPS = 4
EXPERTS_PER_GROUP = 4
D_EXPERT = D_MODEL // 4
EPS = 1e-6

SUBLANES = 8
VMEM_LIMIT = 56 * 1024 * 1024

LOG2E = 1.4426950408889634
Q_SCALE = (SB_HEAD_DIM ** -0.5) * LOG2E

MIX_TILE = 512
ATT_TQ = 256
ATT_TK = 256
ATT_HEADS_PER_STEP = 4
XA_TILE = 512
MOE_TILE = 1024


def _rms(x, gain):
    var = jnp.mean(x * x, axis=-1, keepdims=True)
    return x * lax.rsqrt(var + EPS) * gain


def _dot(a, b):
    return jnp.dot(a, b, preferred_element_type=F32)


def _dot_nt(a, b):
    return lax.dot_general(a, b, (((1,), (1,)), ((), ())), preferred_element_type=F32)


def _mixer_in_kernel(x_ref, g_ref, w_ref, cw_ref, cb_ref, wg_ref, bg_ref, lam_ref, grnn_ref,
                     yr_ref, q_ref, k_ref, v_ref,
                     xpad, a_s, b_s, hcar):
    t = pl.program_id(1)
    tile = x_ref.shape[0]

    @pl.when(t == 0)
    def _():
        xpad[0:SUBLANES, :] = jnp.zeros((SUBLANES, RNN_WIDTH), F32)
        hcar[...] = jnp.zeros_like(hcar)

    h = _rms(x_ref[...], g_ref[...]).astype(BF16)
    pr = _dot(h, w_ref[:, 0:2 * RNN_WIDTH])
    qkv = _dot(h, w_ref[:, 2 * RNN_WIDTH:PROJ_WIDTH])
    q_ref[...] = (qkv[:, 0:SB_WIDTH] * Q_SCALE).astype(BF16)
    k_ref[...] = qkv[:, SB_WIDTH:2 * SB_WIDTH].astype(BF16)
    v_ref[...] = qkv[:, 2 * SB_WIDTH:3 * SB_WIDTH].astype(BF16)

    xpad[SUBLANES:SUBLANES + tile, :] = pr[:, 0:RNN_WIDTH]
    xc = cb_ref[...] + cw_ref[0:1, :] * xpad[SUBLANES - 3:SUBLANES - 3 + tile, :]
    for kk in range(1, CONV_WIDTH):
        off = SUBLANES - (CONV_WIDTH - 1) + kk
        xc = xc + cw_ref[kk:kk + 1, :] * xpad[off:off + tile, :]
    xpad[0:SUBLANES, :] = xpad[tile:tile + SUBLANES, :]

    xcb = xc.astype(BF16)
    half = RNN_WIDTH // 2
    g0 = _dot(xcb[:, 0:half], wg_ref[0])
    g1 = _dot(xcb[:, half:RNN_WIDTH], wg_ref[1])
    r_pre = jnp.concatenate([g0[:, 0:half], g1[:, 0:half]], axis=-1) + bg_ref[0:1, :]
    i_pre = jnp.concatenate([g0[:, half:2 * half], g1[:, half:2 * half]], axis=-1) + bg_ref[1:2, :]
    r = jax.nn.sigmoid(r_pre)
    i = jax.nn.sigmoid(i_pre)
    lam = lam_ref[...]
    neg_lam = -lam
    softplus_neg_lam = jnp.maximum(neg_lam, 0.0) + jnp.log(1.0 + jnp.exp(-jnp.abs(lam)))
    log_a = (-LRU_C) * r * softplus_neg_lam
    a = jnp.exp(log_a)
    b = jnp.sqrt(1.0 - a * a) * (i * xc)
    a_s[...] = a
    b_s[...] = b

    row = lax.broadcasted_iota(jnp.int32, (SUBLANES, RNN_WIDTH), 0)

    def scan_body(gidx, hc):
        r0 = pl.multiple_of(gidx * SUBLANES, SUBLANES)
        ag = a_s[pl.ds(r0, SUBLANES), :]
        bgr = b_s[pl.ds(r0, SUBLANES), :]
        for d in (1, 2, 4):
            a_sh = jnp.where(row >= d, pltpu.roll(ag, d, 0), 1.0)
            b_sh = jnp.where(row >= d, pltpu.roll(bgr, d, 0), 0.0)
            bgr = ag * b_sh + bgr
            ag = ag * a_sh
        hg = ag * hc + bgr
        b_s[pl.ds(r0, SUBLANES), :] = hg
        return hg[SUBLANES - 1:SUBLANES, :]

    hlast = lax.fori_loop(0, tile // SUBLANES, scan_body, hcar[...], unroll=4)
    hcar[...] = hlast

    gate = pr[:, RNN_WIDTH:2 * RNN_WIDTH]
    gelu = 0.5 * gate * (1.0 + jnp.tanh(math.sqrt(2.0 / math.pi) * (gate + 0.044715 * (gate * gate * gate))))
    y = b_s[...] * gelu
    yr_ref[...] = _rms(y, grnn_ref[...]).astype(BF16)


def _mixer_in(x, g, w_in, conv_w, conv_b, w_gates, b_gates, lam, g_rnn):
    bsz, seq, _ = x.shape
    tile = MIX_TILE
    grid = (bsz, seq // tile)
    const2 = lambda b, t: (0, 0)
    tok_spec = lambda width: pl.BlockSpec((None, tile, width), lambda b, t: (b, t, 0))
    out_sds = jax.ShapeDtypeStruct((bsz, seq, SB_WIDTH), BF16)
    return pl.pallas_call(
        _mixer_in_kernel,
        out_shape=(out_sds, out_sds, out_sds, out_sds),
        grid=grid,
        in_specs=[
            tok_spec(D_MODEL),
            pl.BlockSpec((1, D_MODEL), const2),
            pl.BlockSpec((D_MODEL, PROJ_WIDTH), const2),
            pl.BlockSpec((CONV_WIDTH, RNN_WIDTH), const2),
            pl.BlockSpec((1, RNN_WIDTH), const2),
            pl.BlockSpec((2, RNN_WIDTH // 2, RNN_WIDTH), lambda b, t: (0, 0, 0)),
            pl.BlockSpec((2, RNN_WIDTH), const2),
            pl.BlockSpec((1, RNN_WIDTH), const2),
            pl.BlockSpec((1, RNN_WIDTH), const2),
        ],
        out_specs=(tok_spec(RNN_WIDTH), tok_spec(SB_WIDTH), tok_spec(SB_WIDTH), tok_spec(SB_WIDTH)),
        scratch_shapes=[
            pltpu.VMEM((tile + SUBLANES, RNN_WIDTH), F32),
            pltpu.VMEM((tile, RNN_WIDTH), F32),
            pltpu.VMEM((tile, RNN_WIDTH), F32),
            pltpu.VMEM((1, RNN_WIDTH), F32),
        ],
        compiler_params=pltpu.CompilerParams(
            dimension_semantics=("arbitrary", "arbitrary"), vmem_limit_bytes=VMEM_LIMIT),
        name="mixer_in",
    )(x, g, w_in, conv_w, conv_b, w_gates, b_gates, lam, g_rnn)


def _softplus2(z2):
    return jnp.maximum(z2, 0.0) + jnp.log2(1.0 + jnp.exp2(-jnp.abs(z2)))


def _suffix_sum(sp, tri):
    hi = sp.astype(BF16)
    lo = (sp - hi.astype(F32)).astype(BF16)
    return _dot(hi, tri) + _dot(lo, tri)


def _sb_attn_kernel(q_ref, k_ref, v_ref, o_ref, q_s, k_s, v_s, acc_s, car_s):
    seq = q_ref.shape[0]
    heads = q_ref.shape[1] // SB_HEAD_DIM
    n_q = seq // ATT_TQ
    lane_sl = [slice(hh * SB_HEAD_DIM, (hh + 1) * SB_HEAD_DIM) for hh in range(heads)]
    for hh in range(heads):
        q_s[hh] = q_ref[:, lane_sl[hh]]
        k_s[hh] = k_ref[:, lane_sl[hh]]
        v_s[hh] = v_ref[:, lane_sl[hh]]

    krow = lax.broadcasted_iota(jnp.int32, (ATT_TK, ATT_TK), 0)
    kcol = lax.broadcasted_iota(jnp.int32, (ATT_TK, ATT_TK), 1)
    tri = jnp.where(krow >= kcol, 1.0, 0.0).astype(BF16)
    qrow = lax.broadcasted_iota(jnp.int32, (ATT_TQ, ATT_TK), 0)
    qcol = lax.broadcasted_iota(jnp.int32, (ATT_TQ, ATT_TK), 1)
    causal = qcol < qrow

    def q_body(qi, _):
        q0 = pl.multiple_of(qi * ATT_TQ, ATT_TQ)
        qs = [q_s[hh, pl.ds(q0, ATT_TQ), :] for hh in range(heads)]
        hs = range(heads)
        zd = [_dot_nt(qs[hh], k_s[hh, pl.ds(q0, ATT_TK), :]) for hh in hs]
        spd = [jnp.where(causal, _softplus2(zd[hh]), 0.0) for hh in hs]
        cud = [_suffix_sum(spd[hh], tri) for hh in hs]
        wd = [jnp.where(causal, jnp.exp2(zd[hh] - cud[hh]), 0.0).astype(BF16) for hh in hs]
        for hh in hs:
            acc_s[hh] = _dot(wd[hh], v_s[hh, pl.ds(q0, ATT_TK), :])
            car_s[hh] = cud[hh][:, 0:1]

        def kv_body(jj, _):
            k0 = pl.multiple_of((qi - 1 - jj) * ATT_TK, ATT_TK)
            zs = [_dot_nt(qs[hh], k_s[hh, pl.ds(k0, ATT_TK), :]) for hh in hs]
            sps = [_softplus2(zs[hh]) for hh in hs]
            ccs = [_suffix_sum(sps[hh], tri) for hh in hs]
            cars = [car_s[hh] for hh in hs]
            wws = [jnp.exp2(zs[hh] - ccs[hh] - cars[hh]).astype(BF16) for hh in hs]
            pvs = [_dot(wws[hh], v_s[hh, pl.ds(k0, ATT_TK), :]) for hh in hs]
            for hh in hs:
                acc_s[hh] += pvs[hh]
                car_s[hh] = cars[hh] + ccs[hh][:, 0:1]
            return 0

        lax.fori_loop(0, qi, kv_body, 0)
        for hh in range(heads):
            o_ref[pl.ds(q0, ATT_TQ), lane_sl[hh]] = acc_s[hh]
        return 0

    lax.fori_loop(0, n_q, q_body, 0)


def _sb_attn(q, k, v):
    bsz, seq, width = q.shape
    hps = ATT_HEADS_PER_STEP
    lanes = hps * SB_HEAD_DIM
    spec = pl.BlockSpec((None, seq, lanes), lambda b, p: (b, 0, p))
    return pl.pallas_call(
        _sb_attn_kernel,
        out_shape=jax.ShapeDtypeStruct((bsz, seq, width), F32),
        grid=(bsz, width // lanes),
        in_specs=[spec, spec, spec],
        out_specs=spec,
        scratch_shapes=[
            pltpu.VMEM((hps, seq, SB_HEAD_DIM), BF16),
            pltpu.VMEM((hps, seq, SB_HEAD_DIM), BF16),
            pltpu.VMEM((hps, seq, SB_HEAD_DIM), BF16),
            pltpu.VMEM((hps, ATT_TQ, SB_HEAD_DIM), F32),
            pltpu.VMEM((hps, ATT_TQ, 1), F32),
        ],
        compiler_params=pltpu.CompilerParams(
            dimension_semantics=("arbitrary", "arbitrary"), vmem_limit_bytes=VMEM_LIMIT),
        name="sb_attn",
    )(q, k, v)


def _mem_kv_kernel(m_ref, g_ref, wk_ref, wv_ref, k_ref, v_ref):
    hm = _rms(m_ref[...], g_ref[...]).astype(BF16)
    k_ref[...] = _dot(hm, wk_ref[...]).astype(BF16)
    v_ref[...] = _dot(hm, wv_ref[...]).astype(BF16)


def _mem_kv(mem, g, w_k, w_v):
    bsz, n_mem, _ = mem.shape
    const2 = lambda b: (0, 0)
    spec = pl.BlockSpec((None, n_mem, D_MODEL), lambda b: (b, 0, 0))
    sds = jax.ShapeDtypeStruct((bsz, n_mem, D_MODEL), BF16)
    return pl.pallas_call(
        _mem_kv_kernel,
        out_shape=(sds, sds),
        grid=(bsz,),
        in_specs=[spec, pl.BlockSpec((1, D_MODEL), const2),
                  pl.BlockSpec((D_MODEL, D_MODEL), const2), pl.BlockSpec((D_MODEL, D_MODEL), const2)],
        out_specs=(spec, spec),
        compiler_params=pltpu.CompilerParams(
            dimension_semantics=("arbitrary",), vmem_limit_bytes=VMEM_LIMIT),
        name="mem_kv",
    )(mem, g, w_k, w_v)


def _mix_xattn_kernel(x_ref, yr_ref, ysb_ref, gsb_ref, wout_ref, gxa_ref, wq_ref, km_ref, vm_ref, wo_ref,
                      o_ref):
    ysb = _rms(ysb_ref[...], gsb_ref[...]).astype(BF16)
    y = jnp.concatenate([yr_ref[...], ysb], axis=-1)
    x1 = x_ref[...] + _dot(y, wout_ref[...])
    hq = _rms(x1, gxa_ref[...]).astype(BF16)
    q = _dot(hq, wq_ref[...]).astype(BF16)
    scale = XA_HEAD_DIM ** -0.5
    outs = []
    for hd in range(XA_HEADS):
        sl = slice(hd * XA_HEAD_DIM, (hd + 1) * XA_HEAD_DIM)
        s = _dot_nt(q[:, sl], km_ref[:, sl]) * scale
        m = jnp.max(s, axis=-1, keepdims=True)
        p = jnp.exp(s - m)
        den = jnp.sum(p, axis=-1, keepdims=True)
        p = (p / den).astype(BF16)
        outs.append(_dot(p, vm_ref[:, sl]).astype(BF16))
    o = jnp.concatenate(outs, axis=-1)
    o_ref[...] = x1 + _dot(o, wo_ref[...])


def _mix_xattn(x, yr, ysb, g_sb, w_out, g_xa, w_q, kmem, vmem, w_o):
    bsz, seq, _ = x.shape
    n_mem = kmem.shape[1]
    tile = XA_TILE
    const2 = lambda b, t: (0, 0)
    tok_spec = lambda width: pl.BlockSpec((None, tile, width), lambda b, t: (b, t, 0))
    mem_spec = pl.BlockSpec((None, n_mem, D_MODEL), lambda b, t: (b, 0, 0))
    w_spec = pl.BlockSpec((D_MODEL, D_MODEL), const2)
    return pl.pallas_call(
        _mix_xattn_kernel,
        out_shape=jax.ShapeDtypeStruct((bsz, seq, D_MODEL), F32),
        grid=(bsz, seq // tile),
        in_specs=[tok_spec(D_MODEL), tok_spec(RNN_WIDTH), tok_spec(SB_WIDTH),
                  pl.BlockSpec((1, SB_WIDTH), const2), w_spec,
                  pl.BlockSpec((1, D_MODEL), const2), w_spec, mem_spec, mem_spec, w_spec],
        out_specs=tok_spec(D_MODEL),
        compiler_params=pltpu.CompilerParams(
            dimension_semantics=("arbitrary", "arbitrary"), vmem_limit_bytes=VMEM_LIMIT),
        name="mix_xattn",
    )(x, yr, ysb, g_sb, w_out, g_xa, w_q, kmem, vmem, w_o)


N_ROUTE = N_GROUPS + N_GROUPS * EXPERTS_PER_GROUP
ROUTE_PAD = 128


def _first_argmax(vals, lane, width):
    m = jnp.max(vals, axis=-1, keepdims=True)
    idx = jnp.min(jnp.where(vals == m, lane, width), axis=-1, keepdims=True)
    return m, idx


def _moe_final_kernel(x_ref, gmoe_ref, wr_ref, br_ref, wg_ref, wu_ref, wd_ref, gfin_ref,
                      o_ref, h_s, comb_s, acc_s):
    g = pl.program_id(1)

    @pl.when(g == 0)
    def _():
        ht = _rms(x_ref[...], gmoe_ref[...])
        hb = ht.astype(BF16)
        h_s[...] = hb
        h_lo = (ht - hb.astype(F32)).astype(BF16)
        logits = _dot(hb, wr_ref[0]) + _dot(h_lo, wr_ref[0]) + _dot(hb, wr_ref[1])
        logits = logits + br_ref[...]
        lane = lax.broadcasted_iota(jnp.int32, logits.shape, 1)
        neg = -jnp.inf
        gl = jnp.where(lane < N_GROUPS, logits, neg)
        gmax, gidx = _first_argmax(gl, lane, ROUTE_PAD)
        gden = jnp.sum(jnp.exp(gl - gmax), axis=-1, keepdims=True)
        group_p = 1.0 / gden
        e_lo = N_GROUPS + gidx * EXPERTS_PER_GROUP
        in_group = (lane >= e_lo) & (lane < e_lo + EXPERTS_PER_GROUP)
        el = jnp.where(in_group, logits, neg)
        m1, i1 = _first_argmax(el, lane, ROUTE_PAD)
        el2 = jnp.where(lane == i1, neg, el)
        m2, i2 = _first_argmax(el2, lane, ROUTE_PAD)
        e2 = jnp.exp(m2 - m1)
        w1 = group_p / (1.0 + e2)
        w2 = w1 * e2
        comb_s[...] = jnp.where(lane == i1, w1, 0.0) + jnp.where(lane == i2, w2, 0.0)
        acc_s[...] = jnp.zeros_like(acc_s)

    hb = h_s[...]
    lane = lax.broadcasted_iota(jnp.int32, comb_s.shape, 1)
    comb = comb_s[...]
    for e in range(EXPERTS_PER_GROUP):
        col = N_GROUPS + g * EXPERTS_PER_GROUP + e
        cw = jnp.sum(jnp.where(lane == col, comb, 0.0), axis=-1, keepdims=True)
        hg = _dot(hb, wg_ref[e])
        hu = _dot(hb, wu_ref[e])
        hid = (hg * jax.nn.sigmoid(hg)) * hu * cw
        acc_s[...] += _dot(hid.astype(BF16), wd_ref[e])

    @pl.when(g == N_GROUPS - 1)
    def _():
        o_ref[...] = _rms(x_ref[...] + acc_s[...], gfin_ref[...])


def _moe_final(x, g_moe, w_router, b_router, w_gate, w_up, w_down, g_final):
    n_tok = x.shape[0]
    tile = MOE_TILE
    const2 = lambda i, g: (0, 0)
    tok_spec = pl.BlockSpec((tile, D_MODEL), lambda i, g: (i, 0))
    return pl.pallas_call(
        _moe_final_kernel,
        out_shape=jax.ShapeDtypeStruct((n_tok, D_MODEL), F32),
        grid=(n_tok // tile, N_GROUPS),
        in_specs=[
            tok_spec,
            pl.BlockSpec((1, D_MODEL), const2),
            pl.BlockSpec((2, D_MODEL, ROUTE_PAD), lambda i, g: (0, 0, 0)),
            pl.BlockSpec((1, ROUTE_PAD), const2),
            pl.BlockSpec((EXPERTS_PER_GROUP, D_MODEL, D_EXPERT), lambda i, g: (g, 0, 0)),
            pl.BlockSpec((EXPERTS_PER_GROUP, D_MODEL, D_EXPERT), lambda i, g: (g, 0, 0)),
            pl.BlockSpec((EXPERTS_PER_GROUP, D_EXPERT, D_MODEL), lambda i, g: (g, 0, 0)),
            pl.BlockSpec((1, D_MODEL), const2),
        ],
        out_specs=tok_spec,
        scratch_shapes=[
            pltpu.VMEM((tile, D_MODEL), BF16),
            pltpu.VMEM((tile, ROUTE_PAD), F32),
            pltpu.VMEM((tile, D_MODEL), F32),
        ],
        compiler_params=pltpu.CompilerParams(
            dimension_semantics=("arbitrary", "arbitrary"), vmem_limit_bytes=VMEM_LIMIT),
        name="moe_final",
    )(x, g_moe, w_router, b_router, w_gate, w_up, w_down, g_final)


def _block_diag_gates(w_a, w_x):
    half_heads = RNN_HEADS // 2
    half = RNN_WIDTH // 2
    out = jnp.zeros((2, half, 2 * half), F32)
    for c in range(2):
        for hh in range(half_heads):
            r0 = hh * RNN_HEAD_DIM
            out = out.at[c, r0:r0 + RNN_HEAD_DIM, r0:r0 + RNN_HEAD_DIM].set(w_a[c * half_heads + hh])
            out = out.at[c, r0:r0 + RNN_HEAD_DIM, half + r0:half + r0 + RNN_HEAD_DIM].set(w_x[c * half_heads + hh])
    return out.astype(BF16)


def _router_weights(w_group, b_group, w_expert, b_expert):
    w = jnp.concatenate(
        [w_group, jnp.transpose(w_expert, (1, 0, 2)).reshape(D_MODEL, N_GROUPS * EXPERTS_PER_GROUP)], axis=-1)
    w = jnp.pad(w, ((0, 0), (0, ROUTE_PAD - N_ROUTE)))
    hi = w.astype(BF16)
    lo = (w - hi.astype(F32)).astype(BF16)
    b = jnp.concatenate([b_group, b_expert.reshape(-1)])
    b = jnp.pad(b, (0, ROUTE_PAD - N_ROUTE)).reshape(1, ROUTE_PAD)
    return jnp.stack([hi, lo]), b


def _layer(x, mem, p):
    bsz, seq, _ = x.shape
    row = lambda a: a.reshape(1, -1)
    w_gates = _block_diag_gates(p["lru_w_a"], p["lru_w_x"])
    b_gates = jnp.stack([p["lru_b_a"].reshape(-1), p["lru_b_x"].reshape(-1)])
    yr, q, k, v = _mixer_in(x, row(p["norm_mix"]), p["w_in"].astype(BF16), p["conv_w"], row(p["conv_b"]),
                            w_gates, b_gates, row(p["lru_lambda"]), row(p["norm_rnn_out"]))
    ysb = _sb_attn(q, k, v)
    kmem, vmem = _mem_kv(mem, row(p["norm_mem"]), p["xa_w_k"].astype(BF16), p["xa_w_v"].astype(BF16))
    x2 = _mix_xattn(x, yr, ysb, row(p["norm_sb_out"]), p["w_out"].astype(BF16), row(p["norm_xattn"]),
                    p["xa_w_q"].astype(BF16), kmem, vmem, p["xa_w_o"].astype(BF16))
    w_router, b_router = _router_weights(p["w_group_router"], p["b_group_router"],
                                         p["w_expert_router"], p["b_expert_router"])
    ne = N_GROUPS * EXPERTS_PER_GROUP
    out = _moe_final(x2.reshape(bsz * seq, D_MODEL), row(p["norm_moe"]), w_router, b_router,
                     p["w_gate"].reshape(ne, D_MODEL, D_EXPERT).astype(BF16),
                     p["w_up"].reshape(ne, D_MODEL, D_EXPERT).astype(BF16),
                     p["w_down"].reshape(ne, D_EXPERT, D_MODEL).astype(BF16),
                     row(p["norm_final"]))
    return out.reshape(bsz, seq, D_MODEL)


def kernel(x, mem, norm_mix, w_in, conv_w, conv_b, lru_w_a, lru_b_a, lru_w_x, lru_b_x, lru_lambda, norm_rnn_out, norm_sb_out, w_out, norm_xattn, norm_mem, xa_w_q, xa_w_k, xa_w_v, xa_w_o, norm_moe, w_group_router, b_group_router, w_expert_router, b_expert_router, w_gate, w_up, w_down, norm_final):
    depth = norm_mix.shape[0]
    assert depth == 1, "the fused final RMSNorm assumes a single layer"
    l = 0
    params = dict(
        norm_mix=norm_mix[l], w_in=w_in[l], conv_w=conv_w[l], conv_b=conv_b[l],
        lru_w_a=lru_w_a[l], lru_b_a=lru_b_a[l], lru_w_x=lru_w_x[l], lru_b_x=lru_b_x[l],
        lru_lambda=lru_lambda[l], norm_rnn_out=norm_rnn_out[l], norm_sb_out=norm_sb_out[l],
        w_out=w_out[l], norm_xattn=norm_xattn[l], norm_mem=norm_mem[l],
        xa_w_q=xa_w_q[l], xa_w_k=xa_w_k[l], xa_w_v=xa_w_v[l], xa_w_o=xa_w_o[l],
        norm_moe=norm_moe[l], w_group_router=w_group_router[l], b_group_router=b_group_router[l],
        w_expert_router=w_expert_router[l], b_expert_router=b_expert_router[l],
        w_gate=w_gate[l], w_up=w_up[l], w_down=w_down[l], norm_final=norm_final)
    return _layer(x, mem, params)
```

```python
import functools
import math

import jax
import jax.numpy as jnp
from jax import lax
from jax.experimental import pallas as pl
from jax.experimental.pallas import tpu as pltpu

F32 = jnp.float32
BF16 = jnp.bfloat16

D_MODEL = 1024
RNN_WIDTH = 512
RNN_HEADS = 8
RNN_HEAD_DIM = 64
CONV_WIDTH = 4
LRU_C = 8.0
SB_WIDTH = 512
SB_HEADS = 8
SB_HEAD_DIM = 64
PROJ_WIDTH = 2 * RNN_WIDTH + 3 * SB_WIDTH
XA_HEADS = 4
XA_HEAD_DIM = D_MODEL // XA_HEADS
N_GROUPS = 4
EXPERTS_PER_GROUP = 4
D_EXPERT = D_MODEL // 4
EPS = 1e-6

SUBLANES = 8
VMEM_LIMIT = 56 * 1024 * 1024

LOG2E = 1.4426950408889634
Q_SCALE = (SB_HEAD_DIM ** -0.5) * LOG2E

MIX_TILE = 512
ATT_TK = 256
ATT_TQ = 2 * ATT_TK
ATT_HEADS_PER_STEP = 4
ATT_UNDERFLOW_LOG2 = 160.0
XA_TILE = 512
MOE_TILE = 512
MOE_CAP = 160


def _rms(x, gain):
    var = jnp.mean(x * x, axis=-1, keepdims=True)
    return x * lax.rsqrt(var + EPS) * gain


def _dot(a, b):
    return jnp.dot(a, b, preferred_element_type=F32)


def _dot_nt(a, b):
    return lax.dot_general(a, b, (((1,), (1,)), ((), ())), preferred_element_type=F32)


def _mixer_in_kernel(x_ref, g_ref, w_ref, cw_ref, cb_ref, wg_ref, bg_ref, lam_ref, grnn_ref,
                     yr_ref, q_ref, k_ref, v_ref,
                     xpad, a_s, b_s, hcar):
    t = pl.program_id(1)
    tile = x_ref.shape[0]

    @pl.when(t == 0)
    def _():
        xpad[0:SUBLANES, :] = jnp.zeros((SUBLANES, RNN_WIDTH), F32)
        hcar[...] = jnp.zeros_like(hcar)

    h = _rms(x_ref[...], g_ref[...]).astype(BF16)
    pr = _dot(h, w_ref[:, 0:2 * RNN_WIDTH])
    qkv = _dot(h, w_ref[:, 2 * RNN_WIDTH:PROJ_WIDTH])
    q_ref[...] = (qkv[:, 0:SB_WIDTH] * Q_SCALE).astype(BF16)
    k_ref[...] = qkv[:, SB_WIDTH:2 * SB_WIDTH].astype(BF16)
    v_ref[...] = qkv[:, 2 * SB_WIDTH:3 * SB_WIDTH].astype(BF16)

    xpad[SUBLANES:SUBLANES + tile, :] = pr[:, 0:RNN_WIDTH]
    xc = cb_ref[...] + cw_ref[0:1, :] * xpad[SUBLANES - 3:SUBLANES - 3 + tile, :]
    for kk in range(1, CONV_WIDTH):
        off = SUBLANES - (CONV_WIDTH - 1) + kk
        xc = xc + cw_ref[kk:kk + 1, :] * xpad[off:off + tile, :]
    xpad[0:SUBLANES, :] = xpad[tile:tile + SUBLANES, :]

    xcb = xc.astype(BF16)
    half = RNN_WIDTH // 2
    g0 = _dot(xcb[:, 0:half], wg_ref[0])
    g1 = _dot(xcb[:, half:RNN_WIDTH], wg_ref[1])
    r_pre = jnp.concatenate([g0[:, 0:half], g1[:, 0:half]], axis=-1) + bg_ref[0:1, :]
    i_pre = jnp.concatenate([g0[:, half:2 * half], g1[:, half:2 * half]], axis=-1) + bg_ref[1:2, :]
    r = jax.nn.sigmoid(r_pre)
    i = jax.nn.sigmoid(i_pre)
    lam = lam_ref[...]
    neg_lam = -lam
    softplus_neg_lam = jnp.maximum(neg_lam, 0.0) + jnp.log(1.0 + jnp.exp(-jnp.abs(lam)))
    log_a = (-LRU_C) * r * softplus_neg_lam
    a = jnp.exp(log_a)
    b = jnp.sqrt(1.0 - a * a) * (i * xc)
    a_s[...] = a
    b_s[...] = b

    row = lax.broadcasted_iota(jnp.int32, (SUBLANES, RNN_WIDTH), 0)

    def scan_body(gidx, hc):
        r0 = pl.multiple_of(gidx * SUBLANES, SUBLANES)
        ag = a_s[pl.ds(r0, SUBLANES), :]
        bgr = b_s[pl.ds(r0, SUBLANES), :]
        for d in (1, 2, 4):
            a_sh = jnp.where(row >= d, pltpu.roll(ag, d, 0), 1.0)
            b_sh = jnp.where(row >= d, pltpu.roll(bgr, d, 0), 0.0)
            bgr = ag * b_sh + bgr
            ag = ag * a_sh
        hg = ag * hc + bgr
        b_s[pl.ds(r0, SUBLANES), :] = hg
        return hg[SUBLANES - 1:SUBLANES, :]

    hlast = lax.fori_loop(0, tile // SUBLANES, scan_body, hcar[...], unroll=4)
    hcar[...] = hlast

    gate = pr[:, RNN_WIDTH:2 * RNN_WIDTH]
    gelu = 0.5 * gate * (1.0 + jnp.tanh(math.sqrt(2.0 / math.pi) * (gate + 0.044715 * (gate * gate * gate))))
    y = b_s[...] * gelu
    yr_ref[...] = _rms(y, grnn_ref[...]).astype(BF16)


def _mixer_in(x, g, w_in, conv_w, conv_b, w_gates, b_gates, lam, g_rnn):
    bsz, seq, _ = x.shape
    tile = MIX_TILE
    grid = (bsz, seq // tile)
    const2 = lambda b, t: (0, 0)
    tok_spec = lambda width: pl.BlockSpec((None, tile, width), lambda b, t: (b, t, 0))
    out_sds = jax.ShapeDtypeStruct((bsz, seq, SB_WIDTH), BF16)
    return pl.pallas_call(
        _mixer_in_kernel,
        out_shape=(out_sds, out_sds, out_sds, out_sds),
        grid=grid,
        in_specs=[
            tok_spec(D_MODEL),
            pl.BlockSpec((1, D_MODEL), const2),
            pl.BlockSpec((D_MODEL, PROJ_WIDTH), const2),
            pl.BlockSpec((CONV_WIDTH, RNN_WIDTH), const2),
            pl.BlockSpec((1, RNN_WIDTH), const2),
            pl.BlockSpec((2, RNN_WIDTH // 2, RNN_WIDTH), lambda b, t: (0, 0, 0)),
            pl.BlockSpec((2, RNN_WIDTH), const2),
            pl.BlockSpec((1, RNN_WIDTH), const2),
            pl.BlockSpec((1, RNN_WIDTH), const2),
        ],
        out_specs=(tok_spec(RNN_WIDTH), tok_spec(SB_WIDTH), tok_spec(SB_WIDTH), tok_spec(SB_WIDTH)),
        scratch_shapes=[
            pltpu.VMEM((tile + SUBLANES, RNN_WIDTH), F32),
            pltpu.VMEM((tile, RNN_WIDTH), F32),
            pltpu.VMEM((tile, RNN_WIDTH), F32),
            pltpu.VMEM((1, RNN_WIDTH), F32),
        ],
        compiler_params=pltpu.CompilerParams(
            dimension_semantics=("arbitrary", "arbitrary"), vmem_limit_bytes=VMEM_LIMIT),
        name="mixer_in",
    )(x, g, w_in, conv_w, conv_b, w_gates, b_gates, lam, g_rnn)


def _softplus2(z2):
    return jnp.maximum(z2, 0.0) + jnp.log(1.0 + jnp.exp2(-jnp.abs(z2))) * LOG2E


def _suffix_sum(sp, tri):
    return _dot(sp.astype(BF16), tri)


def _sb_attn_kernel(q_ref, k_ref, v_ref, o_ref, q_s, k_s, v_s, acc_s, car_s):
    seq = q_ref.shape[0]
    heads = q_ref.shape[1] // SB_HEAD_DIM
    n_q = seq // ATT_TQ
    lane_sl = [slice(hh * SB_HEAD_DIM, (hh + 1) * SB_HEAD_DIM) for hh in range(heads)]
    for hh in range(heads):
        q_s[hh] = q_ref[:, lane_sl[hh]]
        k_s[hh] = k_ref[:, lane_sl[hh]]
        v_s[hh] = v_ref[:, lane_sl[hh]]

    krow = lax.broadcasted_iota(jnp.int32, (ATT_TK, ATT_TK), 0)
    kcol = lax.broadcasted_iota(jnp.int32, (ATT_TK, ATT_TK), 1)
    tri = jnp.where(krow >= kcol, 1.0, 0.0).astype(BF16)
    qrow = lax.broadcasted_iota(jnp.int32, (ATT_TK, ATT_TK), 0)
    qcol = lax.broadcasted_iota(jnp.int32, (ATT_TK, ATT_TK), 1)
    causal = qcol < qrow
    hs = range(heads)
    zero_car = jnp.zeros((ATT_TK, 1), F32)
    zero_acc = jnp.zeros((ATT_TK, SB_HEAD_DIM), F32)

    def mask_top(x):
        return jnp.concatenate([jnp.where(causal, x[0:ATT_TK], 0.0), x[ATT_TK:ATT_TQ]], axis=0)

    def q_body(qi, _):
        q0 = pl.multiple_of(qi * ATT_TQ, ATT_TQ)
        q1 = pl.multiple_of(q0 + ATT_TK, ATT_TK)
        qs = [q_s[hh, pl.ds(q0, ATT_TQ), :] for hh in hs]
        zu = [_dot_nt(qs[hh][ATT_TK:ATT_TQ], k_s[hh, pl.ds(q1, ATT_TK), :]) for hh in hs]
        spu = [jnp.where(causal, _softplus2(zu[hh]), 0.0) for hh in hs]
        cuu = [_suffix_sum(spu[hh], tri) for hh in hs]
        wu = [jnp.where(causal, jnp.exp2(zu[hh] - cuu[hh]), 0.0).astype(BF16) for hh in hs]
        pvu = [_dot(wu[hh], v_s[hh, pl.ds(q1, ATT_TK), :]) for hh in hs]
        zl = [_dot_nt(qs[hh], k_s[hh, pl.ds(q0, ATT_TK), :]) for hh in hs]
        spl = [mask_top(_softplus2(zl[hh])) for hh in hs]
        cul = [_suffix_sum(spl[hh], tri) for hh in hs]
        car_in = [jnp.concatenate([zero_car, cuu[hh][:, 0:1]], axis=0) for hh in hs]
        wl = [mask_top(jnp.exp2(zl[hh] - cul[hh] - car_in[hh])).astype(BF16) for hh in hs]
        for hh in hs:
            acc_s[hh] = (_dot(wl[hh], v_s[hh, pl.ds(q0, ATT_TK), :])
                         + jnp.concatenate([zero_acc, pvu[hh]], axis=0))
            car_s[hh] = car_in[hh] + cul[hh][:, 0:1]

        def kv_cond(state):
            blk, live = state
            return jnp.logical_and(blk >= 0, live > 0)

        def kv_body(state):
            blk, _ = state
            k0 = pl.multiple_of(blk * ATT_TK, ATT_TK)
            zs = [_dot_nt(qs[hh], k_s[hh, pl.ds(k0, ATT_TK), :]) for hh in hs]
            sps = [_softplus2(zs[hh]) for hh in hs]
            ccs = [_suffix_sum(sps[hh], tri) for hh in hs]
            cars = [car_s[hh] for hh in hs]
            wws = [jnp.exp2(zs[hh] - ccs[hh] - cars[hh]).astype(BF16) for hh in hs]
            pvs = [_dot(wws[hh], v_s[hh, pl.ds(k0, ATT_TK), :]) for hh in hs]
            new_cars = [cars[hh] + ccs[hh][:, 0:1] for hh in hs]
            for hh in hs:
                acc_s[hh] += pvs[hh]
                car_s[hh] = new_cars[hh]
            low = new_cars[0]
            for hh in hs[1:]:
                low = jnp.minimum(low, new_cars[hh])
            live = (jnp.min(low) < ATT_UNDERFLOW_LOG2).astype(jnp.int32)
            return blk - 1, live

        lax.while_loop(kv_cond, kv_body, (2 * qi - 1, jnp.int32(1)))

        for hh in hs:
            o_ref[pl.ds(q0, ATT_TQ), lane_sl[hh]] = acc_s[hh]
        return 0

    lax.fori_loop(0, n_q, q_body, 0)


def _sb_attn(q, k, v):
    bsz, seq, width = q.shape
    hps = ATT_HEADS_PER_STEP
    lanes = hps * SB_HEAD_DIM
    spec = pl.BlockSpec((None, seq, lanes), lambda b, p: (b, 0, p))
    return pl.pallas_call(
        _sb_attn_kernel,
        out_shape=jax.ShapeDtypeStruct((bsz, seq, width), F32),
        grid=(bsz, width // lanes),
        in_specs=[spec, spec, spec],
        out_specs=spec,
        scratch_shapes=[
            pltpu.VMEM((hps, seq, SB_HEAD_DIM), BF16),
            pltpu.VMEM((hps, seq, SB_HEAD_DIM), BF16),
            pltpu.VMEM((hps, seq, SB_HEAD_DIM), BF16),
            pltpu.VMEM((hps, ATT_TQ, SB_HEAD_DIM), F32),
            pltpu.VMEM((hps, ATT_TQ, 1), F32),
        ],
        compiler_params=pltpu.CompilerParams(
            dimension_semantics=("arbitrary", "arbitrary"), vmem_limit_bytes=VMEM_LIMIT),
        name="sb_attn",
    )(q, k, v)


def _mem_kv_kernel(m_ref, g_ref, wk_ref, wv_ref, k_ref, v_ref):
    hm = _rms(m_ref[...], g_ref[...]).astype(BF16)
    k_ref[...] = _dot(hm, wk_ref[...]).astype(BF16)
    v_ref[...] = _dot(hm, wv_ref[...]).astype(BF16)


def _mem_kv(mem, g, w_k, w_v):
    bsz, n_mem, _ = mem.shape
    const2 = lambda b: (0, 0)
    spec = pl.BlockSpec((None, n_mem, D_MODEL), lambda b: (b, 0, 0))
    sds = jax.ShapeDtypeStruct((bsz, n_mem, D_MODEL), BF16)
    return pl.pallas_call(
        _mem_kv_kernel,
        out_shape=(sds, sds),
        grid=(bsz,),
        in_specs=[spec, pl.BlockSpec((1, D_MODEL), const2),
                  pl.BlockSpec((D_MODEL, D_MODEL), const2), pl.BlockSpec((D_MODEL, D_MODEL), const2)],
        out_specs=(spec, spec),
        compiler_params=pltpu.CompilerParams(
            dimension_semantics=("arbitrary",), vmem_limit_bytes=VMEM_LIMIT),
        name="mem_kv",
    )(mem, g, w_k, w_v)


def _mix_xattn_kernel(x_ref, yr_ref, ysb_ref, gsb_ref, wout_ref, gxa_ref, wq_ref, km_ref, vm_ref, wo_ref,
                      o_ref):
    ysb = _rms(ysb_ref[...], gsb_ref[...]).astype(BF16)
    y = jnp.concatenate([yr_ref[...], ysb], axis=-1)
    x1 = x_ref[...] + _dot(y, wout_ref[...])
    hq = _rms(x1, gxa_ref[...]).astype(BF16)
    q = _dot(hq, wq_ref[...]).astype(BF16)
    scale = XA_HEAD_DIM ** -0.5
    outs = []
    for hd in range(XA_HEADS):
        sl = slice(hd * XA_HEAD_DIM, (hd + 1) * XA_HEAD_DIM)
        s = _dot_nt(q[:, sl], km_ref[:, sl]) * scale
        m = jnp.max(s, axis=-1, keepdims=True)
        p = jnp.exp(s - m)
        den = jnp.sum(p, axis=-1, keepdims=True)
        p = (p / den).astype(BF16)
        outs.append(_dot(p, vm_ref[:, sl]).astype(BF16))
    o = jnp.concatenate(outs, axis=-1)
    o_ref[...] = x1 + _dot(o, wo_ref[...])


def _mix_xattn(x, yr, ysb, g_sb, w_out, g_xa, w_q, kmem, vmem, w_o):
    bsz, seq, _ = x.shape
    n_mem = kmem.shape[1]
    tile = XA_TILE
    const2 = lambda b, t: (0, 0)
    tok_spec = lambda width: pl.BlockSpec((None, tile, width), lambda b, t: (b, t, 0))
    mem_spec = pl.BlockSpec((None, n_mem, D_MODEL), lambda b, t: (b, 0, 0))
    w_spec = pl.BlockSpec((D_MODEL, D_MODEL), const2)
    return pl.pallas_call(
        _mix_xattn_kernel,
        out_shape=jax.ShapeDtypeStruct((bsz, seq, D_MODEL), F32),
        grid=(bsz, seq // tile),
        in_specs=[tok_spec(D_MODEL), tok_spec(RNN_WIDTH), tok_spec(SB_WIDTH),
                  pl.BlockSpec((1, SB_WIDTH), const2), w_spec,
                  pl.BlockSpec((1, D_MODEL), const2), w_spec, mem_spec, mem_spec, w_spec],
        out_specs=tok_spec(D_MODEL),
        compiler_params=pltpu.CompilerParams(
            dimension_semantics=("arbitrary", "arbitrary"), vmem_limit_bytes=VMEM_LIMIT),
        name="mix_xattn",
    )(x, yr, ysb, g_sb, w_out, g_xa, w_q, kmem, vmem, w_o)


N_ROUTE = N_GROUPS + N_GROUPS * EXPERTS_PER_GROUP
ROUTE_PAD = 128


def _select_by(idx, rows):
    out = rows[-1]
    for i in range(len(rows) - 2, -1, -1):
        out = jnp.where(idx == i, rows[i], out)
    return out


def _first_argmax_rows(rows):
    m = rows[0]
    for r in rows[1:]:
        m = jnp.maximum(m, r)
    idx = jnp.full(m.shape, len(rows) - 1, jnp.int32)
    for i in range(len(rows) - 2, -1, -1):
        idx = jnp.where(rows[i] == m, i, idx)
    return m, idx


def _moe_final_kernel(x_ref, gmoe_ref, wr_ref, br_ref, wg_ref, wu_ref, wd_ref, gfin_ref,
                      o_ref, acc_s, u_s, sel_s, selc_s, cnt_s):
    tile = x_ref.shape[0]
    cap = MOE_CAP
    ng, ne = N_GROUPS, EXPERTS_PER_GROUP

    @pl.when(pl.program_id(0) == 0)
    def _():
        r = lax.broadcasted_iota(jnp.int32, (tile, tile), 0)
        c = lax.broadcasted_iota(jnp.int32, (tile, tile), 1)
        u_s[...] = jnp.where(r < c, 1.0, 0.0).astype(BF16)

    xt = x_ref[...]
    hb = _rms(xt, gmoe_ref[...]).astype(BF16)
    lt = (_dot(hb, wr_ref[...]) + br_ref[...]).T
    row = lambda i: lt[i:i + 1, :]
    gmax, gidx = _first_argmax_rows([row(g) for g in range(ng)])
    gden = jnp.exp(row(0) - gmax)
    for g in range(1, ng):
        gden = gden + jnp.exp(row(g) - gmax)
    group_p = 1.0 / gden
    el = [_select_by(gidx, [row(ng + g * ne + e) for g in range(ng)]) for e in range(ne)]
    m1, i1 = _first_argmax_rows(el)
    m2, i2 = _first_argmax_rows([jnp.where(i1 == e, -jnp.inf, el[e]) for e in range(ne)])
    e2 = jnp.exp(m2 - m1)
    w1 = group_p / (1.0 + e2)
    w2 = w1 * e2
    comb = [jnp.where(i1 == e, w1, 0.0) + jnp.where(i2 == e, w2, 0.0) for e in range(ne)]

    rid = lax.broadcasted_iota(jnp.int32, (2 * SUBLANES, tile), 0)
    onehot = jnp.where(rid == gidx, 1.0, 0.0)
    before = _dot(onehot.astype(BF16), u_s[...])
    pos = _select_by(gidx, [before[g:g + 1, :] for g in range(ng)])
    cw = jnp.zeros((2 * SUBLANES, tile), F32)
    for e in range(ne):
        cw = jnp.where(rid == e, comb[e], cw)
    cw_hi = cw.astype(BF16)
    cw_lo = (cw - cw_hi.astype(F32)).astype(BF16)
    rid_wide = lax.broadcasted_iota(jnp.int32, (ROUTE_PAD, tile), 0)
    sel_rows = jnp.where(rid_wide == gidx, pos, -1.0)
    sel_cols = sel_rows.T

    slot_r = lax.broadcasted_iota(jnp.int32, (cap, tile), 0).astype(F32)
    slot_c = lax.broadcasted_iota(jnp.int32, (tile, cap), 1).astype(F32)

    def gather_rows(sel_r, base):
        gather = jnp.where(sel_r == slot_r + base, 1.0, 0.0).astype(BF16)
        xc = _dot(gather, hb).astype(BF16)
        wc = _dot_nt(gather, cw_hi) + _dot_nt(gather, cw_lo)
        return xc, wc

    def experts(xcs, wcs, firsts):
        pairs = [(i, e) for i in range(len(xcs)) for e in range(ne)]
        hgs = [_dot(xcs[i], wg_ref[firsts[i] + e]) for i, e in pairs]
        hus = [_dot(xcs[i], wu_ref[firsts[i] + e]) for i, e in pairs]
        hids = [((hgs[n] * jax.nn.sigmoid(hgs[n])) * hus[n] * wcs[i][:, e:e + 1]).astype(BF16)
                for n, (i, e) in enumerate(pairs)]
        downs = [_dot(hids[n], wd_ref[firsts[i] + e]) for n, (i, e) in enumerate(pairs)]
        ys = []
        for i in range(len(xcs)):
            y = downs[i * ne]
            for e in range(1, ne):
                y = y + downs[i * ne + e]
            ys.append(y.astype(BF16))
        return ys

    def scatter_rows(sel_c, base, y):
        scatter = jnp.where(sel_c == slot_c + base, 1.0, 0.0).astype(BF16)
        return _dot(scatter, y)

    groups = range(ng)
    rows_in = [gather_rows(sel_rows[g:g + 1, :], 0.0) for g in groups]
    ys = experts([r[0] for r in rows_in], [r[1] for r in rows_in], [g * ne for g in groups])
    moe = scatter_rows(sel_cols[:, 0:1], 0.0, ys[0])
    for g in groups[1:]:
        moe = moe + scatter_rows(sel_cols[:, g:g + 1], 0.0, ys[g])
    acc_s[...] = moe

    sel_s[...] = sel_rows
    for g in groups:
        selc_s[g] = jnp.broadcast_to(sel_cols[:, g:g + 1], (tile, ROUTE_PAD))
        cnt_s[g] = jnp.sum(jnp.where(gidx == g, 1, 0))

    def overflow_group(g, _):
        count = cnt_s[g]
        extra = jnp.int32(0)
        for c in range(1, -(-tile // cap)):
            extra = extra + (count > c * cap).astype(jnp.int32)
        sel_r = sel_s[pl.ds(g, 1), :]
        sel_c = selc_s[g][:, 0:1]

        def chunk(c, _):
            base = (c * cap).astype(F32)
            xc, wc = gather_rows(sel_r, base)
            acc_s[...] += scatter_rows(sel_c, base, experts([xc], [wc], [g * ne])[0])
            return 0

        lax.fori_loop(1, 1 + extra, chunk, 0)
        return 0

    lax.fori_loop(0, ng, overflow_group, 0)

    o_ref[...] = _rms(xt + acc_s[...], gfin_ref[...])


def _moe_final(x, g_moe, w_router, b_router, w_gate, w_up, w_down, g_final):
    n_tok = x.shape[0]
    tile = MOE_TILE
    n_exp = w_gate.shape[0]
    const2 = lambda i: (0, 0)
    const3 = lambda i: (0, 0, 0)
    resident = pl.Buffered(1)
    tok_spec = pl.BlockSpec((tile, D_MODEL), lambda i: (i, 0))
    return pl.pallas_call(
        _moe_final_kernel,
        out_shape=jax.ShapeDtypeStruct((n_tok, D_MODEL), F32),
        grid=(n_tok // tile,),
        in_specs=[
            tok_spec,
            pl.BlockSpec((1, D_MODEL), const2),
            pl.BlockSpec((D_MODEL, ROUTE_PAD), const2),
            pl.BlockSpec((1, ROUTE_PAD), const2),
            pl.BlockSpec((n_exp, D_MODEL, D_EXPERT), const3, pipeline_mode=resident),
            pl.BlockSpec((n_exp, D_MODEL, D_EXPERT), const3, pipeline_mode=resident),
            pl.BlockSpec((n_exp, D_EXPERT, D_MODEL), const3, pipeline_mode=resident),
            pl.BlockSpec((1, D_MODEL), const2),
        ],
        out_specs=tok_spec,
        scratch_shapes=[
            pltpu.VMEM((tile, D_MODEL), F32),
            pltpu.VMEM((tile, tile), BF16),
            pltpu.VMEM((ROUTE_PAD, tile), F32),
            pltpu.VMEM((N_GROUPS, tile, ROUTE_PAD), F32),
            pltpu.SMEM((N_GROUPS,), jnp.int32),
        ],
        compiler_params=pltpu.CompilerParams(
            dimension_semantics=("arbitrary",), vmem_limit_bytes=VMEM_LIMIT),
        name="moe_final",
    )(x, g_moe, w_router, b_router, w_gate, w_up, w_down, g_final)


def _block_diag_gates(w_a, w_x):
    half_heads = RNN_HEADS // 2
    half = RNN_WIDTH // 2
    out = jnp.zeros((2, half, 2 * half), F32)
    for c in range(2):
        for hh in range(half_heads):
            r0 = hh * RNN_HEAD_DIM
            out = out.at[c, r0:r0 + RNN_HEAD_DIM, r0:r0 + RNN_HEAD_DIM].set(w_a[c * half_heads + hh])
            out = out.at[c, r0:r0 + RNN_HEAD_DIM, half + r0:half + r0 + RNN_HEAD_DIM].set(w_x[c * half_heads + hh])
    return out.astype(BF16)


def _router_weights(w_group, b_group, w_expert, b_expert):
    w = jnp.concatenate(
        [w_group, jnp.transpose(w_expert, (1, 0, 2)).reshape(D_MODEL, N_GROUPS * EXPERTS_PER_GROUP)], axis=-1)
    w = jnp.pad(w, ((0, 0), (0, ROUTE_PAD - N_ROUTE)))
    b = jnp.concatenate([b_group, b_expert.reshape(-1)])
    b = jnp.pad(b, (0, ROUTE_PAD - N_ROUTE)).reshape(1, ROUTE_PAD)
    return w.astype(BF16), b


def _layer(x, mem, p):
    bsz, seq, _ = x.shape
    row = lambda a: a.reshape(1, -1)
    w_gates = _block_diag_gates(p["lru_w_a"], p["lru_w_x"])
    b_gates = jnp.stack([p["lru_b_a"].reshape(-1), p["lru_b_x"].reshape(-1)])
    yr, q, k, v = _mixer_in(x, row(p["norm_mix"]), p["w_in"].astype(BF16), p["conv_w"], row(p["conv_b"]),
                            w_gates, b_gates, row(p["lru_lambda"]), row(p["norm_rnn_out"]))
    ysb = _sb_attn(q, k, v)
    kmem, vmem = _mem_kv(mem, row(p["norm_mem"]), p["xa_w_k"].astype(BF16), p["xa_w_v"].astype(BF16))
    x2 = _mix_xattn(x, yr, ysb, row(p["norm_sb_out"]), p["w_out"].astype(BF16), row(p["norm_xattn"]),
                    p["xa_w_q"].astype(BF16), kmem, vmem, p["xa_w_o"].astype(BF16))
    w_router, b_router = _router_weights(p["w_group_router"], p["b_group_router"],
                                         p["w_expert_router"], p["b_expert_router"])
    ne = N_GROUPS * EXPERTS_PER_GROUP
    out = _moe_final(x2.reshape(bsz * seq, D_MODEL), row(p["norm_moe"]), w_router, b_router,
                     p["w_gate"].reshape(ne, D_MODEL, D_EXPERT).astype(BF16),
                     p["w_up"].reshape(ne, D_MODEL, D_EXPERT).astype(BF16),
                     p["w_down"].reshape(ne, D_EXPERT, D_MODEL).astype(BF16),
                     row(p["norm_final"]))
    return out.reshape(bsz, seq, D_MODEL)


def kernel(x, mem, norm_mix, w_in, conv_w, conv_b, lru_w_a, lru_b_a, lru_w_x, lru_b_x, lru_lambda, norm_rnn_out, norm_sb_out, w_out, norm_xattn, norm_mem, xa_w_q, xa_w_k, xa_w_v, xa_w_o, norm_moe, w_group_router, b_group_router, w_expert_router, b_expert_router, w_gate, w_up, w_down, norm_final):
    depth = norm_mix.shape[0]
    assert depth == 1, "the fused final RMSNorm assumes a single layer"
    l = 0
    params = dict(
        norm_mix=norm_mix[l], w_in=w_in[l], conv_w=conv_w[l], conv_b=conv_b[l],
        lru_w_a=lru_w_a[l], lru_b_a=lru_b_a[l], lru_w_x=lru_w_x[l], lru_b_x=lru_b_x[l],
        lru_lambda=lru_lambda[l], norm_rnn_out=norm_rnn_out[l], norm_sb_out=norm_sb_out[l],
        w_out=w_out[l], norm_xattn=norm_xattn[l], norm_mem=norm_mem[l],
        xa_w_q=xa_w_q[l], xa_w_k=xa_w_k[l], xa_w_v=xa_w_v[l], xa_w_o=xa_w_o[l],
        norm_moe=norm_moe[l], w_group_router=w_group_router[l], b_group_router=b_group_router[l],
        w_expert_router=w_expert_router[l], b_expert_router=b_expert_router[l],
        w_gate=w_gate[l], w_up=w_up[l], w_down=w_down[l], norm_final=norm_final)
    return _layer(x, mem, params)
```

```python
import functools
import math

import jax
import jax.numpy as jnp
from jax import lax
from jax.experimental import pallas as pl
from jax.experimental.pallas import tpu as pltpu

F32 = jnp.float32
BF16 = jnp.bfloat16

D_MODEL = 1024
RNN_WIDTH = 512
RNN_HEADS = 8
RNN_HEAD_DIM = 64
CONV_WIDTH = 4
LRU_C = 8.0
SB_WIDTH = 512
SB_HEADS = 8
SB_HEAD_DIM = 64
PROJ_WIDTH = 2 * RNN_WIDTH + 3 * SB_WIDTH
XA_HEADS = 4
XA_HEAD_DIM = D_MODEL // XA_HEADS
N_GROUPS = 4
EXPERTS_PER_GROUP = 4
D_EXPERT = D_MODEL // 4
EPS = 1e-6

SUBLANES = 8
VMEM_LIMIT = 56 * 1024 * 1024

LOG2E = 1.4426950408889634
Q_SCALE = (SB_HEAD_DIM ** -0.5) * LOG2E

MIX_TILE = 1024
MIX_SUB = 128
ATT_TK = 256
ATT_TQ = 2 * ATT_TK
ATT_HEADS_PER_STEP = 4
ATT_UNDERFLOW_LOG2 = 160.0
XA_TILE = 1024
XA_SUB = 512
MOE_TILE = 512
MOE_CAP = 160


def _rms(x, gain):
    var = jnp.mean(x * x, axis=-1, keepdims=True)
    return x * lax.rsqrt(var + EPS) * gain


def _dot(a, b):
    return jnp.dot(a, b, preferred_element_type=F32)


def _dot_nt(a, b):
    return lax.dot_general(a, b, (((1,), (1,)), ((), ())), preferred_element_type=F32)


def _mixer_in_kernel(x_ref, g_ref, w_ref, cw_ref, cb_ref, wg_ref, bg_ref, lam_ref, grnn_ref,
                     yr_ref, q_ref, k_ref, v_ref,
                     xpad, hcar):
    t = pl.program_id(1)
    tile = x_ref.shape[0]
    sub = MIX_SUB
    half = RNN_WIDTH // 2

    @pl.when(t == 0)
    def _():
        xpad[0:SUBLANES, :] = jnp.zeros((SUBLANES, RNN_WIDTH), F32)
        hcar[...] = jnp.zeros_like(hcar)

    lam = lam_ref[...]
    softplus_neg_lam = jnp.maximum(-lam, 0.0) + jnp.log(1.0 + jnp.exp(-jnp.abs(lam)))
    row = lax.broadcasted_iota(jnp.int32, (SUBLANES, RNN_WIDTH), 0)

    def project(s):
        rows = slice(s * sub, (s + 1) * sub)
        h = _rms(x_ref[rows, :], g_ref[...]).astype(BF16)
        pr = _dot(h, w_ref[:, 0:2 * RNN_WIDTH])
        qkv = _dot(h, w_ref[:, 2 * RNN_WIDTH:PROJ_WIDTH])
        for hd in range(SB_HEADS):
            c0 = hd * SB_HEAD_DIM
            q_ref[hd, rows, :] = (qkv[:, c0:c0 + SB_HEAD_DIM] * Q_SCALE).astype(BF16)
            k_ref[hd, rows, :] = qkv[:, SB_WIDTH + c0:SB_WIDTH + c0 + SB_HEAD_DIM].astype(BF16)
            v_ref[hd, rows, :] = qkv[:, 2 * SB_WIDTH + c0:2 * SB_WIDTH + c0 + SB_HEAD_DIM].astype(BF16)
        return pr

    def rnn_branch(s, pr):
        rows = slice(s * sub, (s + 1) * sub)
        xpad[SUBLANES:SUBLANES + sub, :] = pr[:, 0:RNN_WIDTH]
        xc = cb_ref[...] + cw_ref[0:1, :] * xpad[SUBLANES - 3:SUBLANES - 3 + sub, :]
        for kk in range(1, CONV_WIDTH):
            off = SUBLANES - (CONV_WIDTH - 1) + kk
            xc = xc + cw_ref[kk:kk + 1, :] * xpad[off:off + sub, :]
        xpad[0:SUBLANES, :] = xpad[sub:sub + SUBLANES, :]

        xcb = xc.astype(BF16)
        g0 = _dot(xcb[:, 0:half], wg_ref[0])
        g1 = _dot(xcb[:, half:RNN_WIDTH], wg_ref[1])
        r_pre = jnp.concatenate([g0[:, 0:half], g1[:, 0:half]], axis=-1) + bg_ref[0:1, :]
        i_pre = jnp.concatenate([g0[:, half:2 * half], g1[:, half:2 * half]], axis=-1) + bg_ref[1:2, :]
        r = jax.nn.sigmoid(r_pre)
        i = jax.nn.sigmoid(i_pre)
        a = jnp.exp((-LRU_C) * r * softplus_neg_lam)
        b = jnp.sqrt(1.0 - a * a) * (i * xc)

        hc = hcar[...]
        hseq = []
        for gi in range(sub // SUBLANES):
            ag = a[gi * SUBLANES:(gi + 1) * SUBLANES, :]
            bgr = b[gi * SUBLANES:(gi + 1) * SUBLANES, :]
            for d in (1, 2, 4):
                a_sh = jnp.where(row >= d, pltpu.roll(ag, d, 0), 1.0)
                b_sh = jnp.where(row >= d, pltpu.roll(bgr, d, 0), 0.0)
                bgr = ag * b_sh + bgr
                ag = ag * a_sh
            hg = ag * hc + bgr
            hseq.append(hg)
            hc = hg[SUBLANES - 1:SUBLANES, :]
        hcar[...] = hc

        gate = pr[:, RNN_WIDTH:2 * RNN_WIDTH]
        gelu = 0.5 * gate * (1.0 + jnp.tanh(math.sqrt(2.0 / math.pi) * (gate + 0.044715 * (gate * gate * gate))))
        y = jnp.concatenate(hseq, axis=0) * gelu
        yr_ref[rows, :] = _rms(y, grnn_ref[...]).astype(BF16)

    n_sub = tile // sub
    pending = project(0)
    for s in range(n_sub):
        nxt = project(s + 1) if s + 1 < n_sub else None
        rnn_branch(s, pending)
        pending = nxt


def _mixer_in(x, g, w_in, conv_w, conv_b, w_gates, b_gates, lam, g_rnn):
    bsz, seq, _ = x.shape
    tile = MIX_TILE
    grid = (bsz, seq // tile)
    const2 = lambda b, t: (0, 0)
    tok_spec = lambda width: pl.BlockSpec((None, tile, width), lambda b, t: (b, t, 0))
    yr_sds = jax.ShapeDtypeStruct((bsz, seq, RNN_WIDTH), BF16)
    head_sds = jax.ShapeDtypeStruct((bsz, SB_HEADS, seq, SB_HEAD_DIM), BF16)
    head_spec = pl.BlockSpec((None, SB_HEADS, tile, SB_HEAD_DIM), lambda b, t: (b, 0, t, 0))
    return pl.pallas_call(
        _mixer_in_kernel,
        out_shape=(yr_sds, head_sds, head_sds, head_sds),
        grid=grid,
        in_specs=[
            tok_spec(D_MODEL),
            pl.BlockSpec((1, D_MODEL), const2),
            pl.BlockSpec((D_MODEL, PROJ_WIDTH), const2),
            pl.BlockSpec((CONV_WIDTH, RNN_WIDTH), const2),
            pl.BlockSpec((1, RNN_WIDTH), const2),
            pl.BlockSpec((2, RNN_WIDTH // 2, RNN_WIDTH), lambda b, t: (0, 0, 0)),
            pl.BlockSpec((2, RNN_WIDTH), const2),
            pl.BlockSpec((1, RNN_WIDTH), const2),
            pl.BlockSpec((1, RNN_WIDTH), const2),
        ],
        out_specs=(tok_spec(RNN_WIDTH), head_spec, head_spec, head_spec),
        scratch_shapes=[
            pltpu.VMEM((MIX_SUB + SUBLANES, RNN_WIDTH), F32),
            pltpu.VMEM((1, RNN_WIDTH), F32),
        ],
        compiler_params=pltpu.CompilerParams(
            dimension_semantics=("arbitrary", "arbitrary"), vmem_limit_bytes=VMEM_LIMIT),
        name="mixer_in",
    )(x, g, w_in, conv_w, conv_b, w_gates, b_gates, lam, g_rnn)


def _softplus2(z2):
    return jnp.maximum(z2, 0.0) + jnp.log(1.0 + jnp.exp2(-jnp.abs(z2))) * LOG2E


def _suffix_sum(sp, tri):
    return _dot(sp.astype(BF16), tri)


def _sb_attn_kernel(q_s, k_s, v_s, o_ref, acc_s, car_s):
    heads, seq, _ = q_s.shape
    n_q = seq // ATT_TQ
    lane_sl = [slice(hh * SB_HEAD_DIM, (hh + 1) * SB_HEAD_DIM) for hh in range(heads)]

    krow = lax.broadcasted_iota(jnp.int32, (ATT_TK, ATT_TK), 0)
    kcol = lax.broadcasted_iota(jnp.int32, (ATT_TK, ATT_TK), 1)
    tri = jnp.where(krow >= kcol, 1.0, 0.0).astype(BF16)
    qrow = lax.broadcasted_iota(jnp.int32, (ATT_TK, ATT_TK), 0)
    qcol = lax.broadcasted_iota(jnp.int32, (ATT_TK, ATT_TK), 1)
    causal = qcol < qrow
    hs = range(heads)
    zero_car = jnp.zeros((ATT_TK, 1), F32)
    zero_acc = jnp.zeros((ATT_TK, SB_HEAD_DIM), F32)

    def mask_top(x):
        return jnp.concatenate([jnp.where(causal, x[0:ATT_TK], 0.0), x[ATT_TK:ATT_TQ]], axis=0)

    def q_body(qi, _):
        q0 = pl.multiple_of(qi * ATT_TQ, ATT_TQ)
        q1 = pl.multiple_of(q0 + ATT_TK, ATT_TK)
        qs = [q_s[hh, pl.ds(q0, ATT_TQ), :] for hh in hs]
        zu = [_dot_nt(qs[hh][ATT_TK:ATT_TQ], k_s[hh, pl.ds(q1, ATT_TK), :]) for hh in hs]
        spu = [jnp.where(causal, _softplus2(zu[hh]), 0.0) for hh in hs]
        cuu = [_suffix_sum(spu[hh], tri) for hh in hs]
        wu = [jnp.where(causal, jnp.exp2(zu[hh] - cuu[hh]), 0.0).astype(BF16) for hh in hs]
        pvu = [_dot(wu[hh], v_s[hh, pl.ds(q1, ATT_TK), :]) for hh in hs]
        zl = [_dot_nt(qs[hh], k_s[hh, pl.ds(q0, ATT_TK), :]) for hh in hs]
        spl = [mask_top(_softplus2(zl[hh])) for hh in hs]
        cul = [_suffix_sum(spl[hh], tri) for hh in hs]
        car_in = [jnp.concatenate([zero_car, cuu[hh][:, 0:1]], axis=0) for hh in hs]
        wl = [mask_top(jnp.exp2(zl[hh] - cul[hh] - car_in[hh])).astype(BF16) for hh in hs]
        for hh in hs:
            acc_s[hh] = (_dot(wl[hh], v_s[hh, pl.ds(q0, ATT_TK), :])
                         + jnp.concatenate([zero_acc, pvu[hh]], axis=0))
            car_s[hh] = car_in[hh] + cul[hh][:, 0:1]

        def kv_cond(state):
            blk, live = state
            return jnp.logical_and(blk >= 0, live > 0)

        def kv_body(state):
            blk, _ = state
            k0 = pl.multiple_of(blk * ATT_TK, ATT_TK)
            zs = [_dot_nt(qs[hh], k_s[hh, pl.ds(k0, ATT_TK), :]) for hh in hs]
            sps = [_softplus2(zs[hh]) for hh in hs]
            ccs = [_suffix_sum(sps[hh], tri) for hh in hs]
            cars = [car_s[hh] for hh in hs]
            wws = [jnp.exp2(zs[hh] - ccs[hh] - cars[hh]).astype(BF16) for hh in hs]
            pvs = [_dot(wws[hh], v_s[hh, pl.ds(k0, ATT_TK), :]) for hh in hs]
            new_cars = [cars[hh] + ccs[hh][:, 0:1] for hh in hs]
            for hh in hs:
                acc_s[hh] += pvs[hh]
                car_s[hh] = new_cars[hh]
            low = new_cars[0]
            for hh in hs[1:]:
                low = jnp.minimum(low, new_cars[hh])
            live = (jnp.min(low) < ATT_UNDERFLOW_LOG2).astype(jnp.int32)
            return blk - 1, live

        lax.while_loop(kv_cond, kv_body, (2 * qi - 1, jnp.int32(1)))

        for hh in hs:
            o_ref[pl.ds(q0, ATT_TQ), lane_sl[hh]] = acc_s[hh]
        return 0

    lax.fori_loop(0, n_q, q_body, 0)


def _sb_attn(q, k, v):
    bsz, n_heads, seq, _ = q.shape
    hps = ATT_HEADS_PER_STEP
    lanes = hps * SB_HEAD_DIM
    in_spec = pl.BlockSpec((None, hps, seq, SB_HEAD_DIM), lambda b, p: (b, p, 0, 0))
    return pl.pallas_call(
        _sb_attn_kernel,
        out_shape=jax.ShapeDtypeStruct((bsz, seq, n_heads * SB_HEAD_DIM), F32),
        grid=(bsz, n_heads // hps),
        in_specs=[in_spec, in_spec, in_spec],
        out_specs=pl.BlockSpec((None, seq, lanes), lambda b, p: (b, 0, p)),
        scratch_shapes=[
            pltpu.VMEM((hps, ATT_TQ, SB_HEAD_DIM), F32),
            pltpu.VMEM((hps, ATT_TQ, 1), F32),
        ],
        compiler_params=pltpu.CompilerParams(
            dimension_semantics=("arbitrary", "arbitrary"), vmem_limit_bytes=VMEM_LIMIT),
        name="sb_attn",
    )(q, k, v)


def _mem_kv_kernel(m_ref, g_ref, wk_ref, wv_ref, k_ref, v_ref):
    hm = _rms(m_ref[...], g_ref[...]).astype(BF16)
    k_ref[...] = _dot(hm, wk_ref[...]).astype(BF16)
    v_ref[...] = _dot(hm, wv_ref[...]).astype(BF16)


def _mem_kv(mem, g, w_k, w_v):
    bsz, n_mem, _ = mem.shape
    const2 = lambda b: (0, 0)
    spec = pl.BlockSpec((None, n_mem, D_MODEL), lambda b: (b, 0, 0))
    sds = jax.ShapeDtypeStruct((bsz, n_mem, D_MODEL), BF16)
    return pl.pallas_call(
        _mem_kv_kernel,
        out_shape=(sds, sds),
        grid=(bsz,),
        in_specs=[spec, pl.BlockSpec((1, D_MODEL), const2),
                  pl.BlockSpec((D_MODEL, D_MODEL), const2), pl.BlockSpec((D_MODEL, D_MODEL), const2)],
        out_specs=(spec, spec),
        compiler_params=pltpu.CompilerParams(
            dimension_semantics=("arbitrary",), vmem_limit_bytes=VMEM_LIMIT),
        name="mem_kv",
    )(mem, g, w_k, w_v)


def _mix_xattn_kernel(x_ref, yr_ref, ysb_ref, gsb_ref, wout_ref, gxa_ref, wq_ref, km_ref, vm_ref, wo_ref,
                      o_ref):
    tile = x_ref.shape[0]
    parts = [slice(r0, r0 + XA_SUB) for r0 in range(0, tile, XA_SUB)]
    scale = XA_HEAD_DIM ** -0.5
    ys = [jnp.concatenate([yr_ref[r, :], _rms(ysb_ref[r, :], gsb_ref[...]).astype(BF16)], axis=-1)
          for r in parts]
    x1s = [x_ref[r, :] + _dot(y, wout_ref[...]) for r, y in zip(parts, ys)]
    qs = [_dot(_rms(x1, gxa_ref[...]).astype(BF16), wq_ref[...]).astype(BF16) for x1 in x1s]
    os = []
    for q in qs:
        outs = []
        for hd in range(XA_HEADS):
            sl = slice(hd * XA_HEAD_DIM, (hd + 1) * XA_HEAD_DIM)
            s = _dot_nt(q[:, sl], km_ref[:, sl]) * scale
            m = jnp.max(s, axis=-1, keepdims=True)
            p = jnp.exp(s - m)
            den = jnp.sum(p, axis=-1, keepdims=True)
            p = (p / den).astype(BF16)
            outs.append(_dot(p, vm_ref[:, sl]).astype(BF16))
        os.append(jnp.concatenate(outs, axis=-1))
    for r, x1, o in zip(parts, x1s, os):
        o_ref[r, :] = x1 + _dot(o, wo_ref[...])


def _mix_xattn(x, yr, ysb, g_sb, w_out, g_xa, w_q, kmem, vmem, w_o):
    bsz, seq, _ = x.shape
    n_mem = kmem.shape[1]
    tile = XA_TILE
    const2 = lambda b, t: (0, 0)
    tok_spec = lambda width: pl.BlockSpec((None, tile, width), lambda b, t: (b, t, 0))
    mem_spec = pl.BlockSpec((None, n_mem, D_MODEL), lambda b, t: (b, 0, 0))
    w_spec = pl.BlockSpec((D_MODEL, D_MODEL), const2)
    return pl.pallas_call(
        _mix_xattn_kernel,
        out_shape=jax.ShapeDtypeStruct((bsz, seq, D_MODEL), F32),
        grid=(bsz, seq // tile),
        in_specs=[tok_spec(D_MODEL), tok_spec(RNN_WIDTH), tok_spec(SB_WIDTH),
                  pl.BlockSpec((1, SB_WIDTH), const2), w_spec,
                  pl.BlockSpec((1, D_MODEL), const2), w_spec, mem_spec, mem_spec, w_spec],
        out_specs=tok_spec(D_MODEL),
        compiler_params=pltpu.CompilerParams(
            dimension_semantics=("arbitrary", "arbitrary"), vmem_limit_bytes=VMEM_LIMIT),
        name="mix_xattn",
    )(x, yr, ysb, g_sb, w_out, g_xa, w_q, kmem, vmem, w_o)


N_ROUTE = N_GROUPS + N_GROUPS * EXPERTS_PER_GROUP
ROUTE_PAD = 128


def _select_by(idx, rows):
    out = rows[-1]
    for i in range(len(rows) - 2, -1, -1):
        out = jnp.where(idx == i, rows[i], out)
    return out


def _first_argmax_rows(rows):
    m = rows[0]
    for r in rows[1:]:
        m = jnp.maximum(m, r)
    idx = jnp.full(m.shape, len(rows) - 1, jnp.int32)
    for i in range(len(rows) - 2, -1, -1):
        idx = jnp.where(rows[i] == m, i, idx)
    return m, idx


def _moe_final_kernel(x_ref, gmoe_ref, wr_ref, br_ref, wg_ref, wu_ref, wd_ref, gfin_ref,
                      o_ref, acc_s, u_s, sel_s, selc_s, cnt_s):
    tile = x_ref.shape[0]
    cap = MOE_CAP
    ng, ne = N_GROUPS, EXPERTS_PER_GROUP

    @pl.when(pl.program_id(0) == 0)
    def _():
        r = lax.broadcasted_iota(jnp.int32, (tile, tile), 0)
        c = lax.broadcasted_iota(jnp.int32, (tile, tile), 1)
        u_s[...] = jnp.where(r < c, 1.0, 0.0).astype(BF16)

    xt = x_ref[...]
    hb = _rms(xt, gmoe_ref[...]).astype(BF16)
    lt = (_dot(hb, wr_ref[...]) + br_ref[...]).T
    row = lambda i: lt[i:i + 1, :]
    gmax, gidx = _first_argmax_rows([row(g) for g in range(ng)])
    gden = jnp.exp(row(0) - gmax)
    for g in range(1, ng):
        gden = gden + jnp.exp(row(g) - gmax)
    group_p = 1.0 / gden
    el = [_select_by(gidx, [row(ng + g * ne + e) for g in range(ng)]) for e in range(ne)]
    m1, i1 = _first_argmax_rows(el)
    m2, i2 = _first_argmax_rows([jnp.where(i1 == e, -jnp.inf, el[e]) for e in range(ne)])
    e2 = jnp.exp(m2 - m1)
    w1 = group_p / (1.0 + e2)
    w2 = w1 * e2
    comb = [jnp.where(i1 == e, w1, 0.0) + jnp.where(i2 == e, w2, 0.0) for e in range(ne)]

    rid = lax.broadcasted_iota(jnp.int32, (2 * SUBLANES, tile), 0)
    onehot = jnp.where(rid == gidx, 1.0, 0.0)
    before = _dot(onehot.astype(BF16), u_s[...])
    pos = _select_by(gidx, [before[g:g + 1, :] for g in range(ng)])
    cw = jnp.zeros((2 * SUBLANES, tile), F32)
    for e in range(ne):
        cw = jnp.where(rid == e, comb[e], cw)
    cw_hi = cw.astype(BF16)
    cw_lo = (cw - cw_hi.astype(F32)).astype(BF16)
    rid_wide = lax.broadcasted_iota(jnp.int32, (ROUTE_PAD, tile), 0)
    sel_rows = jnp.where(rid_wide == gidx, pos, -1.0)
    sel_cols = sel_rows.T

    slot_r = lax.broadcasted_iota(jnp.int32, (cap, tile), 0).astype(F32)
    slot_c = lax.broadcasted_iota(jnp.int32, (tile, cap), 1).astype(F32)

    def gather_rows(sel_r, base):
        gather = jnp.where(sel_r == slot_r + base, 1.0, 0.0).astype(BF16)
        xc = _dot(gather, hb).astype(BF16)
        wc = _dot_nt(gather, cw_hi) + _dot_nt(gather, cw_lo)
        return xc, wc

    def experts(xcs, wcs, firsts):
        pairs = [(i, e) for i in range(len(xcs)) for e in range(ne)]
        hgs = [_dot(xcs[i], wg_ref[firsts[i] + e]) for i, e in pairs]
        hus = [_dot(xcs[i], wu_ref[firsts[i] + e]) for i, e in pairs]
        hids = [((hgs[n] * jax.nn.sigmoid(hgs[n])) * hus[n] * wcs[i][:, e:e + 1]).astype(BF16)
                for n, (i, e) in enumerate(pairs)]
        downs = [_dot(hids[n], wd_ref[firsts[i] + e]) for n, (i, e) in enumerate(pairs)]
        ys = []
        for i in range(len(xcs)):
            y = downs[i * ne]
            for e in range(1, ne):
                y = y + downs[i * ne + e]
            ys.append(y.astype(BF16))
        return ys

    def scatter_rows(sel_c, base, y):
        scatter = jnp.where(sel_c == slot_c + base, 1.0, 0.0).astype(BF16)
        return _dot(scatter, y)

    groups = range(ng)
    rows_in = [gather_rows(sel_rows[g:g + 1, :], 0.0) for g in groups]
    ys = experts([r[0] for r in rows_in], [r[1] for r in rows_in], [g * ne for g in groups])
    moe = scatter_rows(sel_cols[:, 0:1], 0.0, ys[0])
    for g in groups[1:]:
        moe = moe + scatter_rows(sel_cols[:, g:g + 1], 0.0, ys[g])
    acc_s[...] = moe

    sel_s[...] = sel_rows
    for g in groups:
        selc_s[g] = jnp.broadcast_to(sel_cols[:, g:g + 1], (tile, ROUTE_PAD))
        cnt_s[g] = jnp.sum(jnp.where(gidx == g, 1, 0))

    def overflow_group(g, _):
        count = cnt_s[g]
        extra = jnp.int32(0)
        for c in range(1, -(-tile // cap)):
            extra = extra + (count > c * cap).astype(jnp.int32)
        sel_r = sel_s[pl.ds(g, 1), :]
        sel_c = selc_s[g][:, 0:1]

        def chunk(c, _):
            base = (c * cap).astype(F32)
            xc, wc = gather_rows(sel_r, base)
            acc_s[...] += scatter_rows(sel_c, base, experts([xc], [wc], [g * ne])[0])
            return 0

        lax.fori_loop(1, 1 + extra, chunk, 0)
        return 0

    lax.fori_loop(0, ng, overflow_group, 0)

    o_ref[...] = _rms(xt + acc_s[...], gfin_ref[...])


def _moe_final(x, g_moe, w_router, b_router, w_gate, w_up, w_down, g_final):
    n_tok = x.shape[0]
    tile = MOE_TILE
    n_exp = w_gate.shape[0]
    const2 = lambda i: (0, 0)
    const3 = lambda i: (0, 0, 0)
    resident = pl.Buffered(1)
    tok_spec = pl.BlockSpec((tile, D_MODEL), lambda i: (i, 0))
    return pl.pallas_call(
        _moe_final_kernel,
        out_shape=jax.ShapeDtypeStruct((n_tok, D_MODEL), F32),
        grid=(n_tok // tile,),
        in_specs=[
            tok_spec,
            pl.BlockSpec((1, D_MODEL), const2),
            pl.BlockSpec((D_MODEL, ROUTE_PAD), const2),
            pl.BlockSpec((1, ROUTE_PAD), const2),
            pl.BlockSpec((n_exp, D_MODEL, D_EXPERT), const3, pipeline_mode=resident),
            pl.BlockSpec((n_exp, D_MODEL, D_EXPERT), const3, pipeline_mode=resident),
            pl.BlockSpec((n_exp, D_EXPERT, D_MODEL), const3, pipeline_mode=resident),
            pl.BlockSpec((1, D_MODEL), const2),
        ],
        out_specs=tok_spec,
        scratch_shapes=[
            pltpu.VMEM((tile, D_MODEL), F32),
            pltpu.VMEM((tile, tile), BF16),
            pltpu.VMEM((ROUTE_PAD, tile), F32),
            pltpu.VMEM((N_GROUPS, tile, ROUTE_PAD), F32),
            pltpu.SMEM((N_GROUPS,), jnp.int32),
        ],
        compiler_params=pltpu.CompilerParams(
            dimension_semantics=("arbitrary",), vmem_limit_bytes=VMEM_LIMIT),
        name="moe_final",
    )(x, g_moe, w_router, b_router, w_gate, w_up, w_down, g_final)


def _block_diag_gates(w_a, w_x):
    half_heads = RNN_HEADS // 2
    half = RNN_WIDTH // 2
    out = jnp.zeros((2, half, 2 * half), F32)
    for c in range(2):
        for hh in range(half_heads):
            r0 = hh * RNN_HEAD_DIM
            out = out.at[c, r0:r0 + RNN_HEAD_DIM, r0:r0 + RNN_HEAD_DIM].set(w_a[c * half_heads + hh])
            out = out.at[c, r0:r0 + RNN_HEAD_DIM, half + r0:half + r0 + RNN_HEAD_DIM].set(w_x[c * half_heads + hh])
    return out.astype(BF16)


def _router_weights(w_group, b_group, w_expert, b_expert):
    w = jnp.concatenate(
        [w_group, jnp.transpose(w_expert, (1, 0, 2)).reshape(D_MODEL, N_GROUPS * EXPERTS_PER_GROUP)], axis=-1)
    w = jnp.pad(w, ((0, 0), (0, ROUTE_PAD - N_ROUTE)))
    b = jnp.concatenate([b_group, b_expert.reshape(-1)])
    b = jnp.pad(b, (0, ROUTE_PAD - N_ROUTE)).reshape(1, ROUTE_PAD)
    return w.astype(BF16), b


def _layer(x, mem, p):
    bsz, seq, _ = x.shape
    row = lambda a: a.reshape(1, -1)
    w_gates = _block_diag_gates(p["lru_w_a"], p["lru_w_x"])
    b_gates = jnp.stack([p["lru_b_a"].reshape(-1), p["lru_b_x"].reshape(-1)])
    yr, q, k, v = _mixer_in(x, row(p["norm_mix"]), p["w_in"].astype(BF16), p["conv_w"], row(p["conv_b"]),
                            w_gates, b_gates, row(p["lru_lambda"]), row(p["norm_rnn_out"]))
    ysb = _sb_attn(q, k, v)
    kmem, vmem = _mem_kv(mem, row(p["norm_mem"]), p["xa_w_k"].astype(BF16), p["xa_w_v"].astype(BF16))
    x2 = _mix_xattn(x, yr, ysb, row(p["norm_sb_out"]), p["w_out"].astype(BF16), row(p["norm_xattn"]),
                    p["xa_w_q"].astype(BF16), kmem, vmem, p["xa_w_o"].astype(BF16))
    w_router, b_router = _router_weights(p["w_group_router"], p["b_group_router"],
                                         p["w_expert_router"], p["b_expert_router"])
    ne = N_GROUPS * EXPERTS_PER_GROUP
    out = _moe_final(x2.reshape(bsz * seq, D_MODEL), row(p["norm_moe"]), w_router, b_router,
                     p["w_gate"].reshape(ne, D_MODEL, D_EXPERT).astype(BF16),
                     p["w_up"].reshape(ne, D_MODEL, D_EXPERT).astype(BF16),
                     p["w_down"].reshape(ne, D_EXPERT, D_MODEL).astype(BF16),
                     row(p["norm_final"]))
    return out.reshape(bsz, seq, D_MODEL)


def kernel(x, mem, norm_mix, w_in, conv_w, conv_b, lru_w_a, lru_b_a, lru_w_x, lru_b_x, lru_lambda, norm_rnn_out, norm_sb_out, w_out, norm_xattn, norm_mem, xa_w_q, xa_w_k, xa_w_v, xa_w_o, norm_moe, w_group_router, b_group_router, w_expert_router, b_expert_router, w_gate, w_up, w_down, norm_final):
    depth = norm_mix.shape[0]
    assert depth == 1, "the fused final RMSNorm assumes a single layer"
    l = 0
    params = dict(
        norm_mix=norm_mix[l], w_in=w_in[l], conv_w=conv_w[l], conv_b=conv_b[l],
        lru_w_a=lru_w_a[l], lru_b_a=lru_b_a[l], lru_w_x=lru_w_x[l], lru_b_x=lru_b_x[l],
        lru_lambda=lru_lambda[l], norm_rnn_out=norm_rnn_out[l], norm_sb_out=norm_sb_out[l],
        w_out=w_out[l], norm_xattn=norm_xattn[l], norm_mem=norm_mem[l],
        xa_w_q=xa_w_q[l], xa_w_k=xa_w_k[l], xa_w_v=xa_w_v[l], xa_w_o=xa_w_o[l],
        norm_moe=norm_moe[l], w_group_router=w_group_router[l], b_group_router=b_group_router[l],
        w_expert_router=w_expert_router[l], b_expert_router=b_expert_router[l],
        w_gate=w_gate[l], w_up=w_up[l], w_down=w_down[l], norm_final=norm_final)
    return _layer(x, mem, params)
```

```python
import functools
import math

import jax
import jax.numpy as jnp
from jax import lax
from jax.experimental import pallas as pl
from jax.experimental.pallas import tpu as pltpu

F32 = jnp.float32
BF16 = jnp.bfloat16

D_MODEL = 1024
RNN_WIDTH = 512
RNN_HEADS = 8
RNN_HEAD_DIM = 64
CONV_WIDTH = 4
LRU_C = 8.0
SB_WIDTH = 512
SB_HEADS = 8
SB_HEAD_DIM = 64
PROJ_WIDTH = 2 * RNN_WIDTH + 3 * SB_WIDTH
XA_HEADS = 4
XA_HEAD_DIM = D_MODEL // XA_HEADS
N_GROUPS = 4
EXPERTS_PER_GROUP = 4
D_EXPERT = D_MODEL // 4
EPS = 1e-6

SUBLANES = 8
VMEM_LIMIT = 56 * 1024 * 1024

LOG2E = 1.4426950408889634
Q_SCALE = (SB_HEAD_DIM ** -0.5) * LOG2E

MIX_TILE = 1024
MIX_SUB = 128
ATT_TK = 256
ATT_TQ = 2 * ATT_TK
ATT_HEADS_PER_STEP = 4
ATT_UNDERFLOW_LOG2 = 160.0
XA_TILE = 1024
XA_SUB = 512
MOE_TILE = 512
MOE_CAP = 160


def _rms(x, gain):
    var = jnp.mean(x * x, axis=-1, keepdims=True)
    return x * lax.rsqrt(var + EPS) * gain


def _dot(a, b):
    return jnp.dot(a, b, preferred_element_type=F32)


def _dot_nt(a, b):
    return lax.dot_general(a, b, (((1,), (1,)), ((), ())), preferred_element_type=F32)


def _mixer_in_kernel(x_ref, g_ref, w_ref, cw_ref, cb_ref, wg_ref, bg_ref, lam_ref, grnn_ref,
                     yr_ref, q_ref, k_ref, v_ref,
                     xpad, hcar):
    t = pl.program_id(1)
    tile = x_ref.shape[0]
    sub = MIX_SUB
    half = RNN_WIDTH // 2

    @pl.when(t == 0)
    def _():
        xpad[0:SUBLANES, :] = jnp.zeros((SUBLANES, RNN_WIDTH), F32)
        hcar[...] = jnp.zeros_like(hcar)

    lam = lam_ref[...]
    softplus_neg_lam = jnp.maximum(-lam, 0.0) + jnp.log(1.0 + jnp.exp(-jnp.abs(lam)))
    row = lax.broadcasted_iota(jnp.int32, (SUBLANES, RNN_WIDTH), 0)

    def project(s):
        rows = slice(s * sub, (s + 1) * sub)
        h = _rms(x_ref[rows, :], g_ref[...]).astype(BF16)
        pr = _dot(h, w_ref[:, 0:2 * RNN_WIDTH])
        qkv = _dot(h, w_ref[:, 2 * RNN_WIDTH:PROJ_WIDTH])
        for hd in range(SB_HEADS):
            c0 = hd * SB_HEAD_DIM
            q_ref[hd, rows, :] = (qkv[:, c0:c0 + SB_HEAD_DIM] * Q_SCALE).astype(BF16)
            k_ref[hd, rows, :] = qkv[:, SB_WIDTH + c0:SB_WIDTH + c0 + SB_HEAD_DIM].astype(BF16)
            v_ref[hd, rows, :] = qkv[:, 2 * SB_WIDTH + c0:2 * SB_WIDTH + c0 + SB_HEAD_DIM].astype(BF16)
        return pr

    def rnn_branch(s, pr):
        rows = slice(s * sub, (s + 1) * sub)
        xpad[SUBLANES:SUBLANES + sub, :] = pr[:, 0:RNN_WIDTH]
        xc = cb_ref[...] + cw_ref[0:1, :] * xpad[SUBLANES - 3:SUBLANES - 3 + sub, :]
        for kk in range(1, CONV_WIDTH):
            off = SUBLANES - (CONV_WIDTH - 1) + kk
            xc = xc + cw_ref[kk:kk + 1, :] * xpad[off:off + sub, :]
        xpad[0:SUBLANES, :] = xpad[sub:sub + SUBLANES, :]

        xcb = xc.astype(BF16)
        g0 = _dot(xcb[:, 0:half], wg_ref[0])
        g1 = _dot(xcb[:, half:RNN_WIDTH], wg_ref[1])
        r_pre = jnp.concatenate([g0[:, 0:half], g1[:, 0:half]], axis=-1) + bg_ref[0:1, :]
        i_pre = jnp.concatenate([g0[:, half:2 * half], g1[:, half:2 * half]], axis=-1) + bg_ref[1:2, :]
        r = jax.nn.sigmoid(r_pre)
        i = jax.nn.sigmoid(i_pre)
        a = jnp.exp((-LRU_C) * r * softplus_neg_lam)
        b = jnp.sqrt(1.0 - a * a) * (i * xc)

        hc = hcar[...]
        hseq = []
        for gi in range(sub // SUBLANES):
            ag = a[gi * SUBLANES:(gi + 1) * SUBLANES, :]
            bgr = b[gi * SUBLANES:(gi + 1) * SUBLANES, :]
            for d in (1, 2, 4):
                a_sh = jnp.where(row >= d, pltpu.roll(ag, d, 0), 1.0)
                b_sh = jnp.where(row >= d, pltpu.roll(bgr, d, 0), 0.0)
                bgr = ag * b_sh + bgr
                ag = ag * a_sh
            hg = ag * hc + bgr
            hseq.append(hg)
            hc = hg[SUBLANES - 1:SUBLANES, :]
        hcar[...] = hc

        gate = pr[:, RNN_WIDTH:2 * RNN_WIDTH]
        gelu = 0.5 * gate * (1.0 + jnp.tanh(math.sqrt(2.0 / math.pi) * (gate + 0.044715 * (gate * gate * gate))))
        y = jnp.concatenate(hseq, axis=0) * gelu
        yr_ref[rows, :] = _rms(y, grnn_ref[...]).astype(BF16)

    n_sub = tile // sub
    pending = project(0)
    for s in range(n_sub):
        nxt = project(s + 1) if s + 1 < n_sub else None
        rnn_branch(s, pending)
        pending = nxt


def _mixer_in(x, g, w_in, conv_w, conv_b, w_gates, b_gates, lam, g_rnn):
    bsz, seq, _ = x.shape
    tile = MIX_TILE
    grid = (bsz, seq // tile)
    const2 = lambda b, t: (0, 0)
    tok_spec = lambda width: pl.BlockSpec((None, tile, width), lambda b, t: (b, t, 0))
    yr_sds = jax.ShapeDtypeStruct((bsz, seq, RNN_WIDTH), BF16)
    head_sds = jax.ShapeDtypeStruct((bsz, SB_HEADS, seq, SB_HEAD_DIM), BF16)
    head_spec = pl.BlockSpec((None, SB_HEADS, tile, SB_HEAD_DIM), lambda b, t: (b, 0, t, 0))
    return pl.pallas_call(
        _mixer_in_kernel,
        out_shape=(yr_sds, head_sds, head_sds, head_sds),
        grid=grid,
        in_specs=[
            tok_spec(D_MODEL),
            pl.BlockSpec((1, D_MODEL), const2),
            pl.BlockSpec((D_MODEL, PROJ_WIDTH), const2),
            pl.BlockSpec((CONV_WIDTH, RNN_WIDTH), const2),
            pl.BlockSpec((1, RNN_WIDTH), const2),
            pl.BlockSpec((2, RNN_WIDTH // 2, RNN_WIDTH), lambda b, t: (0, 0, 0)),
            pl.BlockSpec((2, RNN_WIDTH), const2),
            pl.BlockSpec((1, RNN_WIDTH), const2),
            pl.BlockSpec((1, RNN_WIDTH), const2),
        ],
        out_specs=(tok_spec(RNN_WIDTH), head_spec, head_spec, head_spec),
        scratch_shapes=[
            pltpu.VMEM((MIX_SUB + SUBLANES, RNN_WIDTH), F32),
            pltpu.VMEM((1, RNN_WIDTH), F32),
        ],
        compiler_params=pltpu.CompilerParams(
            dimension_semantics=("arbitrary", "arbitrary"), vmem_limit_bytes=VMEM_LIMIT),
        name="mixer_in",
    )(x, g, w_in, conv_w, conv_b, w_gates, b_gates, lam, g_rnn)


def _softplus2(z2):
    return jnp.maximum(z2, 0.0) + jnp.log(1.0 + jnp.exp2(-jnp.abs(z2))) * LOG2E


def _suffix_sum(sp, tri):
    return _dot(sp.astype(BF16), tri)


def _sb_attn_kernel(q_s, k_s, v_s, o_ref, acc_s, car_s):
    heads, seq, _ = q_s.shape
    n_q = seq // ATT_TQ
    lane_sl = [slice(hh * SB_HEAD_DIM, (hh + 1) * SB_HEAD_DIM) for hh in range(heads)]

    krow = lax.broadcasted_iota(jnp.int32, (ATT_TK, ATT_TK), 0)
    kcol = lax.broadcasted_iota(jnp.int32, (ATT_TK, ATT_TK), 1)
    tri = jnp.where(krow >= kcol, 1.0, 0.0).astype(BF16)
    qrow = lax.broadcasted_iota(jnp.int32, (ATT_TK, ATT_TK), 0)
    qcol = lax.broadcasted_iota(jnp.int32, (ATT_TK, ATT_TK), 1)
    causal = qcol < qrow
    hs = range(heads)
    zero_car = jnp.zeros((ATT_TK, 1), F32)
    zero_acc = jnp.zeros((ATT_TK, SB_HEAD_DIM), F32)

    def mask_top(x):
        return jnp.concatenate([jnp.where(causal, x[0:ATT_TK], 0.0), x[ATT_TK:ATT_TQ]], axis=0)

    def q_body(qi, _):
        q0 = pl.multiple_of(qi * ATT_TQ, ATT_TQ)
        q1 = pl.multiple_of(q0 + ATT_TK, ATT_TK)
        qs = [q_s[hh, pl.ds(q0, ATT_TQ), :] for hh in hs]
        zu = [_dot_nt(qs[hh][ATT_TK:ATT_TQ], k_s[hh, pl.ds(q1, ATT_TK), :]) for hh in hs]
        spu = [jnp.where(causal, _softplus2(zu[hh]), 0.0) for hh in hs]
        cuu = [_suffix_sum(spu[hh], tri) for hh in hs]
        wu = [jnp.where(causal, jnp.exp2(zu[hh] - cuu[hh]), 0.0).astype(BF16) for hh in hs]
        pvu = [_dot(wu[hh], v_s[hh, pl.ds(q1, ATT_TK), :]) for hh in hs]
        zl = [_dot_nt(qs[hh], k_s[hh, pl.ds(q0, ATT_TK), :]) for hh in hs]
        spl = [mask_top(_softplus2(zl[hh])) for hh in hs]
        cul = [_suffix_sum(spl[hh], tri) for hh in hs]
        car_in = [jnp.concatenate([zero_car, cuu[hh][:, 0:1]], axis=0) for hh in hs]
        wl = [mask_top(jnp.exp2(zl[hh] - cul[hh] - car_in[hh])).astype(BF16) for hh in hs]
        for hh in hs:
            acc_s[hh] = (_dot(wl[hh], v_s[hh, pl.ds(q0, ATT_TK), :])
                         + jnp.concatenate([zero_acc, pvu[hh]], axis=0))
            car_s[hh] = car_in[hh] + cul[hh][:, 0:1]

        def kv_cond(state):
            blk, live = state
            return jnp.logical_and(blk >= 0, live > 0)

        def kv_body(state):
            blk, _ = state
            k0 = pl.multiple_of(blk * ATT_TK, ATT_TK)
            zs = [_dot_nt(qs[hh], k_s[hh, pl.ds(k0, ATT_TK), :]) for hh in hs]
            sps = [_softplus2(zs[hh]) for hh in hs]
            ccs = [_suffix_sum(sps[hh], tri) for hh in hs]
            cars = [car_s[hh] for hh in hs]
            wws = [jnp.exp2(zs[hh] - ccs[hh] - cars[hh]).astype(BF16) for hh in hs]
            pvs = [_dot(wws[hh], v_s[hh, pl.ds(k0, ATT_TK), :]) for hh in hs]
            new_cars = [cars[hh] + ccs[hh][:, 0:1] for hh in hs]
            for hh in hs:
                acc_s[hh] += pvs[hh]
                car_s[hh] = new_cars[hh]
            low = new_cars[0]
            for hh in hs[1:]:
                low = jnp.minimum(low, new_cars[hh])
            live = (jnp.min(low) < ATT_UNDERFLOW_LOG2).astype(jnp.int32)
            return blk - 1, live

        lax.while_loop(kv_cond, kv_body, (2 * qi - 1, jnp.int32(1)))

        for hh in hs:
            o_ref[pl.ds(q0, ATT_TQ), lane_sl[hh]] = acc_s[hh]
        return 0

    lax.fori_loop(0, n_q, q_body, 0)


def _sb_attn(q, k, v):
    bsz, n_heads, seq, _ = q.shape
    hps = ATT_HEADS_PER_STEP
    lanes = hps * SB_HEAD_DIM
    in_spec = pl.BlockSpec((None, hps, seq, SB_HEAD_DIM), lambda b, p: (b, p, 0, 0))
    return pl.pallas_call(
        _sb_attn_kernel,
        out_shape=jax.ShapeDtypeStruct((bsz, seq, n_heads * SB_HEAD_DIM), F32),
        grid=(bsz, n_heads // hps),
        in_specs=[in_spec, in_spec, in_spec],
        out_specs=pl.BlockSpec((None, seq, lanes), lambda b, p: (b, 0, p)),
        scratch_shapes=[
            pltpu.VMEM((hps, ATT_TQ, SB_HEAD_DIM), F32),
            pltpu.VMEM((hps, ATT_TQ, 1), F32),
        ],
        compiler_params=pltpu.CompilerParams(
            dimension_semantics=("arbitrary", "arbitrary"), vmem_limit_bytes=VMEM_LIMIT),
        name="sb_attn",
    )(q, k, v)


def _mem_kv_kernel(m_ref, g_ref, wk_ref, wv_ref, k_ref, v_ref):
    hm = _rms(m_ref[...], g_ref[...]).astype(BF16)
    k_ref[...] = _dot(hm, wk_ref[...]).astype(BF16)
    v_ref[...] = _dot(hm, wv_ref[...]).astype(BF16)


def _mem_kv(mem, g, w_k, w_v):
    bsz, n_mem, _ = mem.shape
    const2 = lambda b: (0, 0)
    spec = pl.BlockSpec((None, n_mem, D_MODEL), lambda b: (b, 0, 0))
    sds = jax.ShapeDtypeStruct((bsz, n_mem, D_MODEL), BF16)
    return pl.pallas_call(
        _mem_kv_kernel,
        out_shape=(sds, sds),
        grid=(bsz,),
        in_specs=[spec, pl.BlockSpec((1, D_MODEL), const2),
                  pl.BlockSpec((D_MODEL, D_MODEL), const2), pl.BlockSpec((D_MODEL, D_MODEL), const2)],
        out_specs=(spec, spec),
        compiler_params=pltpu.CompilerParams(
            dimension_semantics=("arbitrary",), vmem_limit_bytes=VMEM_LIMIT),
        name="mem_kv",
    )(mem, g, w_k, w_v)


def _mix_xattn_kernel(x_ref, yr_ref, ysb_ref, gsb_ref, wout_ref, gxa_ref, wq_ref, km_ref, vm_ref, wo_ref,
                      o_ref):
    tile = x_ref.shape[0]
    parts = [slice(r0, r0 + XA_SUB) for r0 in range(0, tile, XA_SUB)]
    scale = XA_HEAD_DIM ** -0.5
    ys = [jnp.concatenate([yr_ref[r, :], _rms(ysb_ref[r, :], gsb_ref[...]).astype(BF16)], axis=-1)
          for r in parts]
    x1s = [x_ref[r, :] + _dot(y, wout_ref[...]) for r, y in zip(parts, ys)]
    qs = [_dot(_rms(x1, gxa_ref[...]).astype(BF16), wq_ref[...]).astype(BF16) for x1 in x1s]
    os = []
    for q in qs:
        outs = []
        for hd in range(XA_HEADS):
            sl = slice(hd * XA_HEAD_DIM, (hd + 1) * XA_HEAD_DIM)
            s = _dot_nt(q[:, sl], km_ref[:, sl]) * scale
            m = jnp.max(s, axis=-1, keepdims=True)
            p = jnp.exp(s - m)
            den = jnp.sum(p, axis=-1, keepdims=True)
            p = (p / den).astype(BF16)
            outs.append(_dot(p, vm_ref[:, sl]).astype(BF16))
        os.append(jnp.concatenate(outs, axis=-1))
    for r, x1, o in zip(parts, x1s, os):
        o_ref[r, :] = x1 + _dot(o, wo_ref[...])


def _mix_xattn(x, yr, ysb, g_sb, w_out, g_xa, w_q, kmem, vmem, w_o):
    bsz, seq, _ = x.shape
    n_mem = kmem.shape[1]
    tile = XA_TILE
    const2 = lambda b, t: (0, 0)
    tok_spec = lambda width: pl.BlockSpec((None, tile, width), lambda b, t: (b, t, 0))
    mem_spec = pl.BlockSpec((None, n_mem, D_MODEL), lambda b, t: (b, 0, 0))
    w_spec = pl.BlockSpec((D_MODEL, D_MODEL), const2)
    return pl.pallas_call(
        _mix_xattn_kernel,
        out_shape=jax.ShapeDtypeStruct((bsz, seq, D_MODEL), F32),
        grid=(bsz, seq // tile),
        in_specs=[tok_spec(D_MODEL), tok_spec(RNN_WIDTH), tok_spec(SB_WIDTH),
                  pl.BlockSpec((1, SB_WIDTH), const2), w_spec,
                  pl.BlockSpec((1, D_MODEL), const2), w_spec, mem_spec, mem_spec, w_spec],
        out_specs=tok_spec(D_MODEL),
        compiler_params=pltpu.CompilerParams(
            dimension_semantics=("arbitrary", "arbitrary"), vmem_limit_bytes=VMEM_LIMIT),
        name="mix_xattn",
    )(x, yr, ysb, g_sb, w_out, g_xa, w_q, kmem, vmem, w_o)


N_ROUTE = N_GROUPS + N_GROUPS * EXPERTS_PER_GROUP
ROUTE_PAD = 128


def _select_by(idx, rows):
    out = rows[-1]
    for i in range(len(rows) - 2, -1, -1):
        out = jnp.where(idx == i, rows[i], out)
    return out


def _first_argmax_rows(rows):
    m = rows[0]
    for r in rows[1:]:
        m = jnp.maximum(m, r)
    idx = jnp.full(m.shape, len(rows) - 1, jnp.int32)
    for i in range(len(rows) - 2, -1, -1):
        idx = jnp.where(rows[i] == m, i, idx)
    return m, idx


def _moe_final_kernel(x_ref, gmoe_ref, wr_ref, br_ref, wg_ref, wu_ref, wd_ref, gfin_ref,
                      o_ref, acc_s, u_s, sel_s, selc_s, cnt_s):
    tile = x_ref.shape[0]
    cap = MOE_CAP
    ng, ne = N_GROUPS, EXPERTS_PER_GROUP

    @pl.when(pl.program_id(0) == 0)
    def _():
        r = lax.broadcasted_iota(jnp.int32, (tile, tile), 0)
        c = lax.broadcasted_iota(jnp.int32, (tile, tile), 1)
        u_s[...] = jnp.where(r < c, 1.0, 0.0).astype(BF16)

    xt = x_ref[...]
    hb = _rms(xt, gmoe_ref[...]).astype(BF16)
    lt = (_dot(hb, wr_ref[...]) + br_ref[...]).T
    row = lambda i: lt[i:i + 1, :]
    gmax, gidx = _first_argmax_rows([row(g) for g in range(ng)])
    gden = jnp.exp(row(0) - gmax)
    for g in range(1, ng):
        gden = gden + jnp.exp(row(g) - gmax)
    group_p = 1.0 / gden
    el = [_select_by(gidx, [row(ng + g * ne + e) for g in range(ng)]) for e in range(ne)]
    m1, i1 = _first_argmax_rows(el)
    m2, i2 = _first_argmax_rows([jnp.where(i1 == e, -jnp.inf, el[e]) for e in range(ne)])
    e2 = jnp.exp(m2 - m1)
    w1 = group_p / (1.0 + e2)
    w2 = w1 * e2
    comb = [jnp.where(i1 == e, w1, 0.0) + jnp.where(i2 == e, w2, 0.0) for e in range(ne)]

    rid = lax.broadcasted_iota(jnp.int32, (2 * SUBLANES, tile), 0)
    onehot = jnp.where(rid == gidx, 1.0, 0.0)
    before = _dot(onehot.astype(BF16), u_s[...])
    pos = _select_by(gidx, [before[g:g + 1, :] for g in range(ng)])
    cw = jnp.zeros((2 * SUBLANES, tile), F32)
    for e in range(ne):
        cw = jnp.where(rid == e, comb[e], cw)
    cw_hi = cw.astype(BF16)
    cw_lo = (cw - cw_hi.astype(F32)).astype(BF16)
    rid_wide = lax.broadcasted_iota(jnp.int32, (ROUTE_PAD, tile), 0)
    sel_rows = jnp.where(rid_wide == gidx, pos, -1.0)
    sel_cols = sel_rows.T

    slot_r = lax.broadcasted_iota(jnp.int32, (cap, tile), 0).astype(F32)
    slot_c = lax.broadcasted_iota(jnp.int32, (tile, cap), 1).astype(F32)

    def gather_rows(sel_r, base):
        gather = jnp.where(sel_r == slot_r + base, 1.0, 0.0).astype(BF16)
        xc = _dot(gather, hb).astype(BF16)
        wc = _dot_nt(gather, cw_hi) + _dot_nt(gather, cw_lo)
        return xc, wc

    def experts(xcs, wcs, gs):
        pairs = [(i, e) for i in range(len(xcs)) for e in range(ne)]
        hgs = [_dot(xcs[i], wg_ref[gs[i] * ne + e]) for i, e in pairs]
        hus = [_dot(xcs[i], wu_ref[gs[i] * ne + e]) for i, e in pairs]
        hids = [((hgs[n] * jax.nn.sigmoid(hgs[n])) * hus[n] * wcs[i][:, e:e + 1]).astype(BF16)
                for n, (i, e) in enumerate(pairs)]
        return [_dot(jnp.concatenate(hids[i * ne:(i + 1) * ne], axis=-1), wd_ref[gs[i]]).astype(BF16)
                for i in range(len(xcs))]

    def scatter_rows(sel_c, base, y):
        scatter = jnp.where(sel_c == slot_c + base, 1.0, 0.0).astype(BF16)
        return _dot(scatter, y)

    groups = range(ng)
    rows_in = [gather_rows(sel_rows[g:g + 1, :], 0.0) for g in groups]
    ys = experts([r[0] for r in rows_in], [r[1] for r in rows_in], list(groups))
    moe = scatter_rows(sel_cols[:, 0:1], 0.0, ys[0])
    for g in groups[1:]:
        moe = moe + scatter_rows(sel_cols[:, g:g + 1], 0.0, ys[g])
    acc_s[...] = moe

    sel_s[...] = sel_rows
    for g in groups:
        selc_s[g] = jnp.broadcast_to(sel_cols[:, g:g + 1], (tile, ROUTE_PAD))
        cnt_s[g] = jnp.sum(jnp.where(gidx == g, 1, 0))

    def overflow_group(g, _):
        count = cnt_s[g]
        extra = jnp.int32(0)
        for c in range(1, -(-tile // cap)):
            extra = extra + (count > c * cap).astype(jnp.int32)
        sel_r = sel_s[pl.ds(g, 1), :]
        sel_c = selc_s[g][:, 0:1]

        def chunk(c, _):
            base = (c * cap).astype(F32)
            xc, wc = gather_rows(sel_r, base)
            acc_s[...] += scatter_rows(sel_c, base, experts([xc], [wc], [g])[0])
            return 0

        lax.fori_loop(1, 1 + extra, chunk, 0)
        return 0

    lax.fori_loop(0, ng, overflow_group, 0)

    o_ref[...] = _rms(xt + acc_s[...], gfin_ref[...])


def _moe_final(x, g_moe, w_router, b_router, w_gate, w_up, w_down, g_final):
    n_tok = x.shape[0]
    tile = MOE_TILE
    n_exp = w_gate.shape[0]
    const2 = lambda i: (0, 0)
    const3 = lambda i: (0, 0, 0)
    resident = pl.Buffered(1)
    tok_spec = pl.BlockSpec((tile, D_MODEL), lambda i: (i, 0))
    return pl.pallas_call(
        _moe_final_kernel,
        out_shape=jax.ShapeDtypeStruct((n_tok, D_MODEL), F32),
        grid=(n_tok // tile,),
        in_specs=[
            tok_spec,
            pl.BlockSpec((1, D_MODEL), const2),
            pl.BlockSpec((D_MODEL, ROUTE_PAD), const2),
            pl.BlockSpec((1, ROUTE_PAD), const2),
            pl.BlockSpec((n_exp, D_MODEL, D_EXPERT), const3, pipeline_mode=resident),
            pl.BlockSpec((n_exp, D_MODEL, D_EXPERT), const3, pipeline_mode=resident),
            pl.BlockSpec(w_down.shape, const3, pipeline_mode=resident),
            pl.BlockSpec((1, D_MODEL), const2),
        ],
        out_specs=tok_spec,
        scratch_shapes=[
            pltpu.VMEM((tile, D_MODEL), F32),
            pltpu.VMEM((tile, tile), BF16),
            pltpu.VMEM((ROUTE_PAD, tile), F32),
            pltpu.VMEM((N_GROUPS, tile, ROUTE_PAD), F32),
            pltpu.SMEM((N_GROUPS,), jnp.int32),
        ],
        compiler_params=pltpu.CompilerParams(
            dimension_semantics=("arbitrary",), vmem_limit_bytes=VMEM_LIMIT),
        name="moe_final",
    )(x, g_moe, w_router, b_router, w_gate, w_up, w_down, g_final)


def _block_diag_gates(w_a, w_x):
    half_heads = RNN_HEADS // 2
    half = RNN_WIDTH // 2
    out = jnp.zeros((2, half, 2 * half), F32)
    for c in range(2):
        for hh in range(half_heads):
            r0 = hh * RNN_HEAD_DIM
            out = out.at[c, r0:r0 + RNN_HEAD_DIM, r0:r0 + RNN_HEAD_DIM].set(w_a[c * half_heads + hh])
            out = out.at[c, r0:r0 + RNN_HEAD_DIM, half + r0:half + r0 + RNN_HEAD_DIM].set(w_x[c * half_heads + hh])
    return out.astype(BF16)


def _router_weights(w_group, b_group, w_expert, b_expert):
    w = jnp.concatenate(
        [w_group, jnp.transpose(w_expert, (1, 0, 2)).reshape(D_MODEL, N_GROUPS * EXPERTS_PER_GROUP)], axis=-1)
    w = jnp.pad(w, ((0, 0), (0, ROUTE_PAD - N_ROUTE)))
    b = jnp.concatenate([b_group, b_expert.reshape(-1)])
    b = jnp.pad(b, (0, ROUTE_PAD - N_ROUTE)).reshape(1, ROUTE_PAD)
    return w.astype(BF16), b


def _layer(x, mem, p):
    bsz, seq, _ = x.shape
    row = lambda a: a.reshape(1, -1)
    w_gates = _block_diag_gates(p["lru_w_a"], p["lru_w_x"])
    b_gates = jnp.stack([p["lru_b_a"].reshape(-1), p["lru_b_x"].reshape(-1)])
    yr, q, k, v = _mixer_in(x, row(p["norm_mix"]), p["w_in"].astype(BF16), p["conv_w"], row(p["conv_b"]),
                            w_gates, b_gates, row(p["lru_lambda"]), row(p["norm_rnn_out"]))
    ysb = _sb_attn(q, k, v)
    kmem, vmem = _mem_kv(mem, row(p["norm_mem"]), p["xa_w_k"].astype(BF16), p["xa_w_v"].astype(BF16))
    x2 = _mix_xattn(x, yr, ysb, row(p["norm_sb_out"]), p["w_out"].astype(BF16), row(p["norm_xattn"]),
                    p["xa_w_q"].astype(BF16), kmem, vmem, p["xa_w_o"].astype(BF16))
    w_router, b_router = _router_weights(p["w_group_router"], p["b_group_router"],
                                         p["w_expert_router"], p["b_expert_router"])
    ne = N_GROUPS * EXPERTS_PER_GROUP
    out = _moe_final(x2.reshape(bsz * seq, D_MODEL), row(p["norm_moe"]), w_router, b_router,
                     p["w_gate"].reshape(ne, D_MODEL, D_EXPERT).astype(BF16),
                     p["w_up"].reshape(ne, D_MODEL, D_EXPERT).astype(BF16),
                     p["w_down"].reshape(N_GROUPS, EXPERTS_PER_GROUP * D_EXPERT, D_MODEL).astype(BF16),
                     row(p["norm_final"]))
    return out.reshape(bsz, seq, D_MODEL)


def kernel(x, mem, norm_mix, w_in, conv_w, conv_b, lru_w_a, lru_b_a, lru_w_x, lru_b_x, lru_lambda, norm_rnn_out, norm_sb_out, w_out, norm_xattn, norm_mem, xa_w_q, xa_w_k, xa_w_v, xa_w_o, norm_moe, w_group_router, b_group_router, w_expert_router, b_expert_router, w_gate, w_up, w_down, norm_final):
    depth = norm_mix.shape[0]
    assert depth == 1, "the fused final RMSNorm assumes a single layer"
    l = 0
    params = dict(
        norm_mix=norm_mix[l], w_in=w_in[l], conv_w=conv_w[l], conv_b=conv_b[l],
        lru_w_a=lru_w_a[l], lru_b_a=lru_b_a[l], lru_w_x=lru_w_x[l], lru_b_x=lru_b_x[l],
        lru_lambda=lru_lambda[l], norm_rnn_out=norm_rnn_out[l], norm_sb_out=norm_sb_out[l],
        w_out=w_out[l], norm_xattn=norm_xattn[l], norm_mem=norm_mem[l],
        xa_w_q=xa_w_q[l], xa_w_k=xa_w_k[l], xa_w_v=xa_w_v[l], xa_w_o=xa_w_o[l],
        norm_moe=norm_moe[l], w_group_router=w_group_router[l], b_group_router=b_group_router[l],
        w_expert_router=w_expert_router[l], b_expert_router=b_expert_router[l],
        w_gate=w_gate[l], w_up=w_up[l], w_down=w_down[l], norm_final=norm_final)
    return _layer(x, mem, params)
```

```python
import functools
import math

import jax
import jax.numpy as jnp
from jax import lax
from jax.experimental import pallas as pl
from jax.experimental.pallas import tpu as pltpu

F32 = jnp.float32
BF16 = jnp.bfloat16

D_MODEL = 1024
RNN_WIDTH = 512
RNN_HEADS = 8
RNN_HEAD_DIM = 64
CONV_WIDTH = 4
LRU_C = 8.0
SB_WIDTH = 512
SB_HEADS = 8
SB_HEAD_DIM = 64
PROJ_WIDTH = 2 * RNN_WIDTH + 3 * SB_WIDTH
XA_HEADS = 4
XA_HEAD_DIM = D_MODEL // XA_HEADS
N_GROUPS = 4
EXPERTS_PER_GROUP = 4
D_EXPERT = D_MODEL // 4
EPS = 1e-6

SUBLANES = 8
VMEM_LIMIT = 56 * 1024 * 1024

LOG2E = 1.4426950408889634
Q_SCALE = (SB_HEAD_DIM ** -0.5) * LOG2E

MIX_TILE = 1024
MIX_SUB = 128
ATT_TK = 256
ATT_TQ = 2 * ATT_TK
ATT_HEADS_PER_STEP = 4
ATT_UNDERFLOW_LOG2 = 160.0
XA_TILE = 1024
XA_SUB = 512
MOE_TILE = 512
MOE_CAP = 160


def _rms(x, gain):
    var = jnp.mean(x * x, axis=-1, keepdims=True)
    return x * lax.rsqrt(var + EPS) * gain


def _dot(a, b):
    return jnp.dot(a, b, preferred_element_type=F32)


def _dot_nt(a, b):
    return lax.dot_general(a, b, (((1,), (1,)), ((), ())), preferred_element_type=F32)


def _mixer_in_kernel(x_ref, g_ref, w_ref, cw_ref, cb_ref, wg_ref, bg_ref, lam_ref, grnn_ref,
                     yr_ref, q_ref, k_ref, v_ref,
                     xpad, hcar):
    t = pl.program_id(1)
    tile = x_ref.shape[0]
    sub = MIX_SUB
    half = RNN_WIDTH // 2

    @pl.when(t == 0)
    def _():
        xpad[0:SUBLANES, :] = jnp.zeros((SUBLANES, RNN_WIDTH), F32)
        hcar[...] = jnp.zeros_like(hcar)

    lam = lam_ref[...]
    softplus_neg_lam = jnp.maximum(-lam, 0.0) + jnp.log(1.0 + jnp.exp(-jnp.abs(lam)))
    row = lax.broadcasted_iota(jnp.int32, (SUBLANES, RNN_WIDTH), 0)

    def project(s):
        rows = slice(s * sub, (s + 1) * sub)
        h = _rms(x_ref[rows, :], g_ref[...]).astype(BF16)
        pr = _dot(h, w_ref[:, 0:2 * RNN_WIDTH])
        qkv = _dot(h, w_ref[:, 2 * RNN_WIDTH:PROJ_WIDTH])
        for hd in range(SB_HEADS):
            c0 = hd * SB_HEAD_DIM
            q_ref[hd, rows, :] = (qkv[:, c0:c0 + SB_HEAD_DIM] * Q_SCALE).astype(BF16)
            k_ref[hd, rows, :] = qkv[:, SB_WIDTH + c0:SB_WIDTH + c0 + SB_HEAD_DIM].astype(BF16)
            v_ref[hd, rows, :] = qkv[:, 2 * SB_WIDTH + c0:2 * SB_WIDTH + c0 + SB_HEAD_DIM].astype(BF16)
        return pr

    def rnn_branch(s, pr):
        rows = slice(s * sub, (s + 1) * sub)
        xpad[SUBLANES:SUBLANES + sub, :] = pr[:, 0:RNN_WIDTH]
        xc = cb_ref[...] + cw_ref[0:1, :] * xpad[SUBLANES - 3:SUBLANES - 3 + sub, :]
        for kk in range(1, CONV_WIDTH):
            off = SUBLANES - (CONV_WIDTH - 1) + kk
            xc = xc + cw_ref[kk:kk + 1, :] * xpad[off:off + sub, :]
        xpad[0:SUBLANES, :] = xpad[sub:sub + SUBLANES, :]

        xcb = xc.astype(BF16)
        g0 = _dot(xcb[:, 0:half], wg_ref[0])
        g1 = _dot(xcb[:, half:RNN_WIDTH], wg_ref[1])
        r_pre = jnp.concatenate([g0[:, 0:half], g1[:, 0:half]], axis=-1) + bg_ref[0:1, :]
        i_pre = jnp.concatenate([g0[:, half:2 * half], g1[:, half:2 * half]], axis=-1) + bg_ref[1:2, :]
        r = jax.nn.sigmoid(r_pre)
        i = jax.nn.sigmoid(i_pre)
        a = jnp.exp((-LRU_C) * r * softplus_neg_lam)
        b = jnp.sqrt(1.0 - a * a) * (i * xc)

        hc = hcar[...]
        hseq = []
        for gi in range(sub // SUBLANES):
            ag = a[gi * SUBLANES:(gi + 1) * SUBLANES, :]
            bgr = b[gi * SUBLANES:(gi + 1) * SUBLANES, :]
            for d in (1, 2, 4):
                a_sh = jnp.where(row >= d, pltpu.roll(ag, d, 0), 1.0)
                b_sh = jnp.where(row >= d, pltpu.roll(bgr, d, 0), 0.0)
                bgr = ag * b_sh + bgr
                ag = ag * a_sh
            hg = ag * hc + bgr
            hseq.append(hg)
            hc = hg[SUBLANES - 1:SUBLANES, :]
        hcar[...] = hc

        gate = pr[:, RNN_WIDTH:2 * RNN_WIDTH]
        gelu = 0.5 * gate * (1.0 + jnp.tanh(math.sqrt(2.0 / math.pi) * (gate + 0.044715 * (gate * gate * gate))))
        y = jnp.concatenate(hseq, axis=0) * gelu
        yr_ref[rows, :] = _rms(y, grnn_ref[...]).astype(BF16)

    n_sub = tile // sub
    pending = project(0)
    for s in range(n_sub):
        nxt = project(s + 1) if s + 1 < n_sub else None
        rnn_branch(s, pending)
        pending = nxt


def _mixer_in(x, g, w_in, conv_w, conv_b, w_gates, b_gates, lam, g_rnn):
    bsz, seq, _ = x.shape
    tile = MIX_TILE
    grid = (bsz, seq // tile)
    const2 = lambda b, t: (0, 0)
    tok_spec = lambda width: pl.BlockSpec((None, tile, width), lambda b, t: (b, t, 0))
    yr_sds = jax.ShapeDtypeStruct((bsz, seq, RNN_WIDTH), BF16)
    head_sds = jax.ShapeDtypeStruct((bsz, SB_HEADS, seq, SB_HEAD_DIM), BF16)
    head_spec = pl.BlockSpec((None, SB_HEADS, tile, SB_HEAD_DIM), lambda b, t: (b, 0, t, 0))
    return pl.pallas_call(
        _mixer_in_kernel,
        out_shape=(yr_sds, head_sds, head_sds, head_sds),
        grid=grid,
        in_specs=[
            tok_spec(D_MODEL),
            pl.BlockSpec((1, D_MODEL), const2),
            pl.BlockSpec((D_MODEL, PROJ_WIDTH), const2),
            pl.BlockSpec((CONV_WIDTH, RNN_WIDTH), const2),
            pl.BlockSpec((1, RNN_WIDTH), const2),
            pl.BlockSpec((2, RNN_WIDTH // 2, RNN_WIDTH), lambda b, t: (0, 0, 0)),
            pl.BlockSpec((2, RNN_WIDTH), const2),
            pl.BlockSpec((1, RNN_WIDTH), const2),
            pl.BlockSpec((1, RNN_WIDTH), const2),
        ],
        out_specs=(tok_spec(RNN_WIDTH), head_spec, head_spec, head_spec),
        scratch_shapes=[
            pltpu.VMEM((MIX_SUB + SUBLANES, RNN_WIDTH), F32),
            pltpu.VMEM((1, RNN_WIDTH), F32),
        ],
        compiler_params=pltpu.CompilerParams(
            dimension_semantics=("arbitrary", "arbitrary"), vmem_limit_bytes=VMEM_LIMIT),
        name="mixer_in",
    )(x, g, w_in, conv_w, conv_b, w_gates, b_gates, lam, g_rnn)


def _softplus2(z2):
    return jnp.maximum(z2, 0.0) + jnp.log(1.0 + jnp.exp2(-jnp.abs(z2))) * LOG2E


def _suffix_sum(sp, tri):
    return _dot(sp.astype(BF16), tri)


def _sb_attn_kernel(q_s, k_s, v_s, o_ref, acc_s, car_s):
    heads, seq, _ = q_s.shape
    n_q = seq // ATT_TQ
    lane_sl = [slice(hh * SB_HEAD_DIM, (hh + 1) * SB_HEAD_DIM) for hh in range(heads)]

    krow = lax.broadcasted_iota(jnp.int32, (ATT_TK, ATT_TK), 0)
    kcol = lax.broadcasted_iota(jnp.int32, (ATT_TK, ATT_TK), 1)
    tri = jnp.where(krow >= kcol, 1.0, 0.0).astype(BF16)
    qrow = lax.broadcasted_iota(jnp.int32, (ATT_TK, ATT_TK), 0)
    qcol = lax.broadcasted_iota(jnp.int32, (ATT_TK, ATT_TK), 1)
    causal = qcol < qrow
    hs = range(heads)
    zero_car = jnp.zeros((ATT_TK, 1), F32)
    zero_acc = jnp.zeros((ATT_TK, SB_HEAD_DIM), F32)

    def mask_top(x):
        return jnp.concatenate([jnp.where(causal, x[0:ATT_TK], 0.0), x[ATT_TK:ATT_TQ]], axis=0)

    def q_body(qi, _):
        q0 = pl.multiple_of(qi * ATT_TQ, ATT_TQ)
        q1 = pl.multiple_of(q0 + ATT_TK, ATT_TK)
        qs = [q_s[hh, pl.ds(q0, ATT_TQ), :] for hh in hs]
        zu = [_dot_nt(qs[hh][ATT_TK:ATT_TQ], k_s[hh, pl.ds(q1, ATT_TK), :]) for hh in hs]
        spu = [jnp.where(causal, _softplus2(zu[hh]), 0.0) for hh in hs]
        cuu = [_suffix_sum(spu[hh], tri) for hh in hs]
        wu = [jnp.where(causal, jnp.exp2(zu[hh] - cuu[hh]), 0.0).astype(BF16) for hh in hs]
        pvu = [_dot(wu[hh], v_s[hh, pl.ds(q1, ATT_TK), :]) for hh in hs]
        zl = [_dot_nt(qs[hh], k_s[hh, pl.ds(q0, ATT_TK), :]) for hh in hs]
        spl = [mask_top(_softplus2(zl[hh])) for hh in hs]
        cul = [_suffix_sum(spl[hh], tri) for hh in hs]
        car_in = [jnp.concatenate([zero_car, cuu[hh][:, 0:1]], axis=0) for hh in hs]
        wl = [mask_top(jnp.exp2(zl[hh] - cul[hh] - car_in[hh])).astype(BF16) for hh in hs]
        car0 = [car_in[hh] + cul[hh][:, 0:1] for hh in hs]
        pvl = [_dot(wl[hh], v_s[hh, pl.ds(q0, ATT_TK), :]) for hh in hs]
        has_prev = qi > 0
        kp = pl.multiple_of(jnp.maximum(2 * qi - 1, 0) * ATT_TK, ATT_TK)
        zp = [_dot_nt(qs[hh][0:ATT_TK], k_s[hh, pl.ds(kp, ATT_TK), :]) for hh in hs]
        cup = [_suffix_sum(_softplus2(zp[hh]), tri) for hh in hs]
        wp = [jnp.where(has_prev, jnp.exp2(zp[hh] - cup[hh] - car0[hh][0:ATT_TK]), 0.0).astype(BF16)
              for hh in hs]
        pvp = [_dot(wp[hh], v_s[hh, pl.ds(kp, ATT_TK), :]) for hh in hs]
        car_top = [car0[hh][0:ATT_TK] + jnp.where(has_prev, cup[hh][:, 0:1], 0.0) for hh in hs]
        car_bot = [car0[hh][ATT_TK:ATT_TQ] for hh in hs]
        for hh in hs:
            acc_s[hh] = pvl[hh] + jnp.concatenate([pvp[hh], pvu[hh]], axis=0)
            car_s[hh] = jnp.concatenate([car_top[hh], car_bot[hh]], axis=0)

        def any_live(cars):
            low = cars[0]
            for hh in hs[1:]:
                low = jnp.minimum(low, cars[hh])
            return (jnp.min(low) < ATT_UNDERFLOW_LOG2).astype(jnp.int32)

        def walk(lo, blk0, cars0):
            rows = slice(lo, lo + ATT_TK)

            def cond(state):
                blk, live = state
                return jnp.logical_and(blk >= 0, live > 0)

            def body(state):
                blk, _ = state
                k0 = pl.multiple_of(blk * ATT_TK, ATT_TK)
                zs = [_dot_nt(qs[hh][rows], k_s[hh, pl.ds(k0, ATT_TK), :]) for hh in hs]
                ccs = [_suffix_sum(_softplus2(zs[hh]), tri) for hh in hs]
                cars = [car_s[hh, rows, :] for hh in hs]
                wws = [jnp.exp2(zs[hh] - ccs[hh] - cars[hh]).astype(BF16) for hh in hs]
                pvs = [_dot(wws[hh], v_s[hh, pl.ds(k0, ATT_TK), :]) for hh in hs]
                new_cars = [cars[hh] + ccs[hh][:, 0:1] for hh in hs]
                for hh in hs:
                    acc_s[hh, rows, :] += pvs[hh]
                    car_s[hh, rows, :] = new_cars[hh]
                return blk - 1, any_live(new_cars)

            lax.while_loop(cond, body, (blk0, any_live(cars0)))

        walk(0, 2 * qi - 2, car_top)
        walk(ATT_TK, 2 * qi - 1, car_bot)

        for hh in hs:
            o_ref[pl.ds(q0, ATT_TQ), lane_sl[hh]] = acc_s[hh]
        return 0

    lax.fori_loop(0, n_q, q_body, 0)


def _sb_attn(q, k, v):
    bsz, n_heads, seq, _ = q.shape
    hps = ATT_HEADS_PER_STEP
    lanes = hps * SB_HEAD_DIM
    in_spec = pl.BlockSpec((None, hps, seq, SB_HEAD_DIM), lambda b, p: (b, p, 0, 0))
    return pl.pallas_call(
        _sb_attn_kernel,
        out_shape=jax.ShapeDtypeStruct((bsz, seq, n_heads * SB_HEAD_DIM), F32),
        grid=(bsz, n_heads // hps),
        in_specs=[in_spec, in_spec, in_spec],
        out_specs=pl.BlockSpec((None, seq, lanes), lambda b, p: (b, 0, p)),
        scratch_shapes=[
            pltpu.VMEM((hps, ATT_TQ, SB_HEAD_DIM), F32),
            pltpu.VMEM((hps, ATT_TQ, 1), F32),
        ],
        compiler_params=pltpu.CompilerParams(
            dimension_semantics=("arbitrary", "arbitrary"), vmem_limit_bytes=VMEM_LIMIT),
        name="sb_attn",
    )(q, k, v)


def _mem_kv_kernel(m_ref, g_ref, wk_ref, wv_ref, k_ref, v_ref):
    hm = _rms(m_ref[...], g_ref[...]).astype(BF16)
    k_ref[...] = _dot(hm, wk_ref[...]).astype(BF16)
    v_ref[...] = _dot(hm, wv_ref[...]).astype(BF16)


def _mem_kv(mem, g, w_k, w_v):
    bsz, n_mem, _ = mem.shape
    const2 = lambda b: (0, 0)
    spec = pl.BlockSpec((None, n_mem, D_MODEL), lambda b: (b, 0, 0))
    sds = jax.ShapeDtypeStruct((bsz, n_mem, D_MODEL), BF16)
    return pl.pallas_call(
        _mem_kv_kernel,
        out_shape=(sds, sds),
        grid=(bsz,),
        in_specs=[spec, pl.BlockSpec((1, D_MODEL), const2),
                  pl.BlockSpec((D_MODEL, D_MODEL), const2), pl.BlockSpec((D_MODEL, D_MODEL), const2)],
        out_specs=(spec, spec),
        compiler_params=pltpu.CompilerParams(
            dimension_semantics=("arbitrary",), vmem_limit_bytes=VMEM_LIMIT),
        name="mem_kv",
    )(mem, g, w_k, w_v)


def _mix_xattn_kernel(x_ref, yr_ref, ysb_ref, gsb_ref, wout_ref, gxa_ref, wq_ref, km_ref, vm_ref, wo_ref,
                      o_ref):
    tile = x_ref.shape[0]
    parts = [slice(r0, r0 + XA_SUB) for r0 in range(0, tile, XA_SUB)]
    scale = XA_HEAD_DIM ** -0.5
    ys = [jnp.concatenate([yr_ref[r, :], _rms(ysb_ref[r, :], gsb_ref[...]).astype(BF16)], axis=-1)
          for r in parts]
    x1s = [x_ref[r, :] + _dot(y, wout_ref[...]) for r, y in zip(parts, ys)]
    qs = [_dot(_rms(x1, gxa_ref[...]).astype(BF16), wq_ref[...]).astype(BF16) for x1 in x1s]
    os = []
    for q in qs:
        outs = []
        for hd in range(XA_HEADS):
            sl = slice(hd * XA_HEAD_DIM, (hd + 1) * XA_HEAD_DIM)
            s = _dot_nt(q[:, sl], km_ref[:, sl]) * scale
            m = jnp.max(s, axis=-1, keepdims=True)
            p = jnp.exp(s - m)
            den = jnp.sum(p, axis=-1, keepdims=True)
            p = (p / den).astype(BF16)
            outs.append(_dot(p, vm_ref[:, sl]).astype(BF16))
        os.append(jnp.concatenate(outs, axis=-1))
    for r, x1, o in zip(parts, x1s, os):
        o_ref[r, :] = x1 + _dot(o, wo_ref[...])


def _mix_xattn(x, yr, ysb, g_sb, w_out, g_xa, w_q, kmem, vmem, w_o):
    bsz, seq, _ = x.shape
    n_mem = kmem.shape[1]
    tile = XA_TILE
    const2 = lambda b, t: (0, 0)
    tok_spec = lambda width: pl.BlockSpec((None, tile, width), lambda b, t: (b, t, 0))
    mem_spec = pl.BlockSpec((None, n_mem, D_MODEL), lambda b, t: (b, 0, 0))
    w_spec = pl.BlockSpec((D_MODEL, D_MODEL), const2)
    return pl.pallas_call(
        _mix_xattn_kernel,
        out_shape=jax.ShapeDtypeStruct((bsz, seq, D_MODEL), F32),
        grid=(bsz, seq // tile),
        in_specs=[tok_spec(D_MODEL), tok_spec(RNN_WIDTH), tok_spec(SB_WIDTH),
                  pl.BlockSpec((1, SB_WIDTH), const2), w_spec,
                  pl.BlockSpec((1, D_MODEL), const2), w_spec, mem_spec, mem_spec, w_spec],
        out_specs=tok_spec(D_MODEL),
        compiler_params=pltpu.CompilerParams(
            dimension_semantics=("arbitrary", "arbitrary"), vmem_limit_bytes=VMEM_LIMIT),
        name="mix_xattn",
    )(x, yr, ysb, g_sb, w_out, g_xa, w_q, kmem, vmem, w_o)


N_ROUTE = N_GROUPS + N_GROUPS * EXPERTS_PER_GROUP
ROUTE_PAD = 128


def _select_by(idx, rows):
    out = rows[-1]
    for i in range(len(rows) - 2, -1, -1):
        out = jnp.where(idx == i, rows[i], out)
    return out


def _first_argmax_rows(rows):
    m = rows[0]
    for r in rows[1:]:
        m = jnp.maximum(m, r)
    idx = jnp.full(m.shape, len(rows) - 1, jnp.int32)
    for i in range(len(rows) - 2, -1, -1):
        idx = jnp.where(rows[i] == m, i, idx)
    return m, idx


def _moe_final_kernel(x_ref, gmoe_ref, wr_ref, br_ref, wg_ref, wu_ref, wd_ref, gfin_ref,
                      o_ref, acc_s, u_s, sel_s, selc_s, cnt_s):
    tile = x_ref.shape[0]
    cap = MOE_CAP
    ng, ne = N_GROUPS, EXPERTS_PER_GROUP

    @pl.when(pl.program_id(0) == 0)
    def _():
        r = lax.broadcasted_iota(jnp.int32, (tile, tile), 0)
        c = lax.broadcasted_iota(jnp.int32, (tile, tile), 1)
        u_s[...] = jnp.where(r < c, 1.0, 0.0).astype(BF16)

    xt = x_ref[...]
    hb = _rms(xt, gmoe_ref[...]).astype(BF16)
    lt = (_dot(hb, wr_ref[...]) + br_ref[...]).T
    row = lambda i: lt[i:i + 1, :]
    gmax, gidx = _first_argmax_rows([row(g) for g in range(ng)])
    gden = jnp.exp(row(0) - gmax)
    for g in range(1, ng):
        gden = gden + jnp.exp(row(g) - gmax)
    group_p = 1.0 / gden
    el = [_select_by(gidx, [row(ng + g * ne + e) for g in range(ng)]) for e in range(ne)]
    m1, i1 = _first_argmax_rows(el)
    m2, i2 = _first_argmax_rows([jnp.where(i1 == e, -jnp.inf, el[e]) for e in range(ne)])
    e2 = jnp.exp(m2 - m1)
    w1 = group_p / (1.0 + e2)
    w2 = w1 * e2
    comb = [jnp.where(i1 == e, w1, 0.0) + jnp.where(i2 == e, w2, 0.0) for e in range(ne)]

    rid = lax.broadcasted_iota(jnp.int32, (2 * SUBLANES, tile), 0)
    onehot = jnp.where(rid == gidx, 1.0, 0.0)
    before = _dot(onehot.astype(BF16), u_s[...])
    pos = _select_by(gidx, [before[g:g + 1, :] for g in range(ng)])
    cw = jnp.zeros((2 * SUBLANES, tile), F32)
    for e in range(ne):
        cw = jnp.where(rid == e, comb[e], cw)
    cw_hi = cw.astype(BF16)
    cw_lo = (cw - cw_hi.astype(F32)).astype(BF16)
    rid_wide = lax.broadcasted_iota(jnp.int32, (ROUTE_PAD, tile), 0)
    sel_rows = jnp.where(rid_wide == gidx, pos, -1.0)
    sel_cols = sel_rows.T

    slot_r = lax.broadcasted_iota(jnp.int32, (cap, tile), 0).astype(F32)
    slot_c = lax.broadcasted_iota(jnp.int32, (tile, cap), 1).astype(F32)

    def gather_rows(sel_r, base):
        gather = jnp.where(sel_r == slot_r + base, 1.0, 0.0).astype(BF16)
        xc = _dot(gather, hb).astype(BF16)
        wc = _dot_nt(gather, cw_hi) + _dot_nt(gather, cw_lo)
        return xc, wc

    def experts(xcs, wcs, gs):
        pairs = [(i, e) for i in range(len(xcs)) for e in range(ne)]
        hgs = [_dot(xcs[i], wg_ref[gs[i] * ne + e]) for i, e in pairs]
        hus = [_dot(xcs[i], wu_ref[gs[i] * ne + e]) for i, e in pairs]
        hids = [((hgs[n] * jax.nn.sigmoid(hgs[n])) * hus[n] * wcs[i][:, e:e + 1]).astype(BF16)
                for n, (i, e) in enumerate(pairs)]
        return [_dot(jnp.concatenate(hids[i * ne:(i + 1) * ne], axis=-1), wd_ref[gs[i]]).astype(BF16)
                for i in range(len(xcs))]

    def scatter_rows(sel_c, base, y):
        scatter = jnp.where(sel_c == slot_c + base, 1.0, 0.0).astype(BF16)
        return _dot(scatter, y)

    groups = range(ng)
    rows_in = [gather_rows(sel_rows[g:g + 1, :], 0.0) for g in groups]
    ys = experts([r[0] for r in rows_in], [r[1] for r in rows_in], list(groups))
    moe = scatter_rows(sel_cols[:, 0:1], 0.0, ys[0])
    for g in groups[1:]:
        moe = moe + scatter_rows(sel_cols[:, g:g + 1], 0.0, ys[g])
    acc_s[...] = moe

    sel_s[...] = sel_rows
    for g in groups:
        selc_s[g] = jnp.broadcast_to(sel_cols[:, g:g + 1], (tile, ROUTE_PAD))
        cnt_s[g] = jnp.sum(jnp.where(gidx == g, 1, 0))

    def overflow_group(g, _):
        count = cnt_s[g]
        extra = jnp.int32(0)
        for c in range(1, -(-tile // cap)):
            extra = extra + (count > c * cap).astype(jnp.int32)
        sel_r = sel_s[pl.ds(g, 1), :]
        sel_c = selc_s[g][:, 0:1]

        def chunk(c, _):
            base = (c * cap).astype(F32)
            xc, wc = gather_rows(sel_r, base)
            acc_s[...] += scatter_rows(sel_c, base, experts([xc], [wc], [g])[0])
            return 0

        lax.fori_loop(1, 1 + extra, chunk, 0)
        return 0

    lax.fori_loop(0, ng, overflow_group, 0)

    o_ref[...] = _rms(xt + acc_s[...], gfin_ref[...])


def _moe_final(x, g_moe, w_router, b_router, w_gate, w_up, w_down, g_final):
    n_tok = x.shape[0]
    tile = MOE_TILE
    n_exp = w_gate.shape[0]
    const2 = lambda i: (0, 0)
    const3 = lambda i: (0, 0, 0)
    resident = pl.Buffered(1)
    tok_spec = pl.BlockSpec((tile, D_MODEL), lambda i: (i, 0))
    return pl.pallas_call(
        _moe_final_kernel,
        out_shape=jax.ShapeDtypeStruct((n_tok, D_MODEL), F32),
        grid=(n_tok // tile,),
        in_specs=[
            tok_spec,
            pl.BlockSpec((1, D_MODEL), const2),
            pl.BlockSpec((D_MODEL, ROUTE_PAD), const2),
            pl.BlockSpec((1, ROUTE_PAD), const2),
            pl.BlockSpec((n_exp, D_MODEL, D_EXPERT), const3, pipeline_mode=resident),
            pl.BlockSpec((n_exp, D_MODEL, D_EXPERT), const3, pipeline_mode=resident),
            pl.BlockSpec(w_down.shape, const3, pipeline_mode=resident),
            pl.BlockSpec((1, D_MODEL), const2),
        ],
        out_specs=tok_spec,
        scratch_shapes=[
            pltpu.VMEM((tile, D_MODEL), F32),
            pltpu.VMEM((tile, tile), BF16),
            pltpu.VMEM((ROUTE_PAD, tile), F32),
            pltpu.VMEM((N_GROUPS, tile, ROUTE_PAD), F32),
            pltpu.SMEM((N_GROUPS,), jnp.int32),
        ],
        compiler_params=pltpu.CompilerParams(
            dimension_semantics=("arbitrary",), vmem_limit_bytes=VMEM_LIMIT),
        name="moe_final",
    )(x, g_moe, w_router, b_router, w_gate, w_up, w_down, g_final)


def _block_diag_gates(w_a, w_x):
    half_heads = RNN_HEADS // 2
    half = RNN_WIDTH // 2
    out = jnp.zeros((2, half, 2 * half), F32)
    for c in range(2):
        for hh in range(half_heads):
            r0 = hh * RNN_HEAD_DIM
            out = out.at[c, r0:r0 + RNN_HEAD_DIM, r0:r0 + RNN_HEAD_DIM].set(w_a[c * half_heads + hh])
            out = out.at[c, r0:r0 + RNN_HEAD_DIM, half + r0:half + r0 + RNN_HEAD_DIM].set(w_x[c * half_heads + hh])
    return out.astype(BF16)


def _router_weights(w_group, b_group, w_expert, b_expert):
    w = jnp.concatenate(
        [w_group, jnp.transpose(w_expert, (1, 0, 2)).reshape(D_MODEL, N_GROUPS * EXPERTS_PER_GROUP)], axis=-1)
    w = jnp.pad(w, ((0, 0), (0, ROUTE_PAD - N_ROUTE)))
    b = jnp.concatenate([b_group, b_expert.reshape(-1)])
    b = jnp.pad(b, (0, ROUTE_PAD - N_ROUTE)).reshape(1, ROUTE_PAD)
    return w.astype(BF16), b


def _layer(x, mem, p):
    bsz, seq, _ = x.shape
    row = lambda a: a.reshape(1, -1)
    w_gates = _block_diag_gates(p["lru_w_a"], p["lru_w_x"])
    b_gates = jnp.stack([p["lru_b_a"].reshape(-1), p["lru_b_x"].reshape(-1)])
    yr, q, k, v = _mixer_in(x, row(p["norm_mix"]), p["w_in"].astype(BF16), p["conv_w"], row(p["conv_b"]),
                            w_gates, b_gates, row(p["lru_lambda"]), row(p["norm_rnn_out"]))
    ysb = _sb_attn(q, k, v)
    kmem, vmem = _mem_kv(mem, row(p["norm_mem"]), p["xa_w_k"].astype(BF16), p["xa_w_v"].astype(BF16))
    x2 = _mix_xattn(x, yr, ysb, row(p["norm_sb_out"]), p["w_out"].astype(BF16), row(p["norm_xattn"]),
                    p["xa_w_q"].astype(BF16), kmem, vmem, p["xa_w_o"].astype(BF16))
    w_router, b_router = _router_weights(p["w_group_router"], p["b_group_router"],
                                         p["w_expert_router"], p["b_expert_router"])
    ne = N_GROUPS * EXPERTS_PER_GROUP
    out = _moe_final(x2.reshape(bsz * seq, D_MODEL), row(p["norm_moe"]), w_router, b_router,
                     p["w_gate"].reshape(ne, D_MODEL, D_EXPERT).astype(BF16),
                     p["w_up"].reshape(ne, D_MODEL, D_EXPERT).astype(BF16),
                     p["w_down"].reshape(N_GROUPS, EXPERTS_PER_GROUP * D_EXPERT, D_MODEL).astype(BF16),
                     row(p["norm_final"]))
    return out.reshape(bsz, seq, D_MODEL)


def kernel(x, mem, norm_mix, w_in, conv_w, conv_b, lru_w_a, lru_b_a, lru_w_x, lru_b_x, lru_lambda, norm_rnn_out, norm_sb_out, w_out, norm_xattn, norm_mem, xa_w_q, xa_w_k, xa_w_v, xa_w_o, norm_moe, w_group_router, b_group_router, w_expert_router, b_expert_router, w_gate, w_up, w_down, norm_final):
    depth = norm_mix.shape[0]
    assert depth == 1, "the fused final RMSNorm assumes a single layer"
    l = 0
    params = dict(
        norm_mix=norm_mix[l], w_in=w_in[l], conv_w=conv_w[l], conv_b=conv_b[l],
        lru_w_a=lru_w_a[l], lru_b_a=lru_b_a[l], lru_w_x=lru_w_x[l], lru_b_x=lru_b_x[l],
        lru_lambda=lru_lambda[l], norm_rnn_out=norm_rnn_out[l], norm_sb_out=norm_sb_out[l],
        w_out=w_out[l], norm_xattn=norm_xattn[l], norm_mem=norm_mem[l],
        xa_w_q=xa_w_q[l], xa_w_k=xa_w_k[l], xa_w_v=xa_w_v[l], xa_w_o=xa_w_o[l],
        norm_moe=norm_moe[l], w_group_router=w_group_router[l], b_group_router=b_group_router[l],
        w_expert_router=w_expert_router[l], b_expert_router=b_expert_router[l],
        w_gate=w_gate[l], w_up=w_up[l], w_down=w_down[l], norm_final=norm_final)
    return _layer(x, mem, params)
```

```python
import functools
import math

import jax
import jax.numpy as jnp
from jax import lax
from jax.experimental import pallas as pl
from jax.experimental.pallas import tpu as pltpu

F32 = jnp.float32
BF16 = jnp.bfloat16

D_MODEL = 1024
RNN_WIDTH = 512
RNN_HEADS = 8
RNN_HEAD_DIM = 64
CONV_WIDTH = 4
LRU_C = 8.0
SB_WIDTH = 512
SB_HEADS = 8
SB_HEAD_DIM = 64
PROJ_WIDTH = 2 * RNN_WIDTH + 3 * SB_WIDTH
XA_HEADS = 4
XA_HEAD_DIM = D_MODEL // XA_HEADS
N_GROUPS = 4
EXPERTS_PER_GROUP = 4
D_EXPERT = D_MODEL // 4
EPS = 1e-6

SUBLANES = 8
VMEM_LIMIT = 56 * 1024 * 1024

LOG2E = 1.4426950408889634
Q_SCALE = (SB_HEAD_DIM ** -0.5) * LOG2E

MIX_TILE = 1024
MIX_SUB = 128
ATT_TK = 256
ATT_TQ = 2 * ATT_TK
ATT_HEADS_PER_STEP = 4
ATT_UNDERFLOW_LOG2 = 160.0
XA_TILE = 1024
XA_SUB = 512
MOE_TILE = 512
MOE_CAP = 160


def _rms(x, gain):
    var = jnp.mean(x * x, axis=-1, keepdims=True)
    return x * lax.rsqrt(var + EPS) * gain


def _dot(a, b):
    return jnp.dot(a, b, preferred_element_type=F32)


def _dot_nt(a, b):
    return lax.dot_general(a, b, (((1,), (1,)), ((), ())), preferred_element_type=F32)


def _mixer_in_kernel(x_ref, g_ref, w_ref, cw_ref, cb_ref, wg_ref, bg_ref, lam_ref, grnn_ref,
                     yr_ref, q_ref, k_ref, v_ref,
                     xpad, hcar):
    t = pl.program_id(1)
    tile = x_ref.shape[0]
    sub = MIX_SUB
    half = RNN_WIDTH // 2

    @pl.when(t == 0)
    def _():
        xpad[0:SUBLANES, :] = jnp.zeros((SUBLANES, RNN_WIDTH), F32)
        hcar[...] = jnp.zeros_like(hcar)

    lam = lam_ref[...]
    softplus_neg_lam = jnp.maximum(-lam, 0.0) + jnp.log(1.0 + jnp.exp(-jnp.abs(lam)))
    row = lax.broadcasted_iota(jnp.int32, (SUBLANES, RNN_WIDTH), 0)

    def project(s):
        rows = slice(s * sub, (s + 1) * sub)
        h = _rms(x_ref[rows, :], g_ref[...]).astype(BF16)
        pr = _dot(h, w_ref[:, 0:2 * RNN_WIDTH])
        qkv = _dot(h, w_ref[:, 2 * RNN_WIDTH:PROJ_WIDTH])
        for hd in range(SB_HEADS):
            c0 = hd * SB_HEAD_DIM
            q_ref[hd, rows, :] = (qkv[:, c0:c0 + SB_HEAD_DIM] * Q_SCALE).astype(BF16)
            k_ref[hd, rows, :] = qkv[:, SB_WIDTH + c0:SB_WIDTH + c0 + SB_HEAD_DIM].astype(BF16)
            v_ref[hd, rows, :] = qkv[:, 2 * SB_WIDTH + c0:2 * SB_WIDTH + c0 + SB_HEAD_DIM].astype(BF16)
        return pr

    def rnn_branch(s, pr):
        rows = slice(s * sub, (s + 1) * sub)
        xpad[SUBLANES:SUBLANES + sub, :] = pr[:, 0:RNN_WIDTH]
        xc = cb_ref[...] + cw_ref[0:1, :] * xpad[SUBLANES - 3:SUBLANES - 3 + sub, :]
        for kk in range(1, CONV_WIDTH):
            off = SUBLANES - (CONV_WIDTH - 1) + kk
            xc = xc + cw_ref[kk:kk + 1, :] * xpad[off:off + sub, :]
        xpad[0:SUBLANES, :] = xpad[sub:sub + SUBLANES, :]

        xcb = xc.astype(BF16)
        g0 = _dot(xcb[:, 0:half], wg_ref[0])
        g1 = _dot(xcb[:, half:RNN_WIDTH], wg_ref[1])
        r_pre = jnp.concatenate([g0[:, 0:half], g1[:, 0:half]], axis=-1) + bg_ref[0:1, :]
        i_pre = jnp.concatenate([g0[:, half:2 * half], g1[:, half:2 * half]], axis=-1) + bg_ref[1:2, :]
        r = jax.nn.sigmoid(r_pre)
        i = jax.nn.sigmoid(i_pre)
        a = jnp.exp((-LRU_C) * r * softplus_neg_lam)
        b = jnp.sqrt(1.0 - a * a) * (i * xc)

        hc = hcar[...]
        hseq = []
        for gi in range(sub // SUBLANES):
            ag = a[gi * SUBLANES:(gi + 1) * SUBLANES, :]
            bgr = b[gi * SUBLANES:(gi + 1) * SUBLANES, :]
            for d in (1, 2, 4):
                a_sh = jnp.where(row >= d, pltpu.roll(ag, d, 0), 1.0)
                b_sh = jnp.where(row >= d, pltpu.roll(bgr, d, 0), 0.0)
                bgr = ag * b_sh + bgr
                ag = ag * a_sh
            hg = ag * hc + bgr
            hseq.append(hg)
            hc = hg[SUBLANES - 1:SUBLANES, :]
        hcar[...] = hc

        gate = pr[:, RNN_WIDTH:2 * RNN_WIDTH]
        gelu = 0.5 * gate * (1.0 + jnp.tanh(math.sqrt(2.0 / math.pi) * (gate + 0.044715 * (gate * gate * gate))))
        y = jnp.concatenate(hseq, axis=0) * gelu
        yr_ref[rows, :] = _rms(y, grnn_ref[...]).astype(BF16)

    n_sub = tile // sub
    pending = project(0)
    for s in range(n_sub):
        nxt = project(s + 1) if s + 1 < n_sub else None
        rnn_branch(s, pending)
        pending = nxt


def _mixer_in(x, g, w_in, conv_w, conv_b, w_gates, b_gates, lam, g_rnn):
    bsz, seq, _ = x.shape
    tile = MIX_TILE
    grid = (bsz, seq // tile)
    const2 = lambda b, t: (0, 0)
    tok_spec = lambda width: pl.BlockSpec((None, tile, width), lambda b, t: (b, t, 0))
    yr_sds = jax.ShapeDtypeStruct((bsz, seq, RNN_WIDTH), BF16)
    head_sds = jax.ShapeDtypeStruct((bsz, SB_HEADS, seq, SB_HEAD_DIM), BF16)
    head_spec = pl.BlockSpec((None, SB_HEADS, tile, SB_HEAD_DIM), lambda b, t: (b, 0, t, 0))
    return pl.pallas_call(
        _mixer_in_kernel,
        out_shape=(yr_sds, head_sds, head_sds, head_sds),
        grid=grid,
        in_specs=[
            tok_spec(D_MODEL),
            pl.BlockSpec((1, D_MODEL), const2),
            pl.BlockSpec((D_MODEL, PROJ_WIDTH), const2),
            pl.BlockSpec((CONV_WIDTH, RNN_WIDTH), const2),
            pl.BlockSpec((1, RNN_WIDTH), const2),
            pl.BlockSpec((2, RNN_WIDTH // 2, RNN_WIDTH), lambda b, t: (0, 0, 0)),
            pl.BlockSpec((2, RNN_WIDTH), const2),
            pl.BlockSpec((1, RNN_WIDTH), const2),
            pl.BlockSpec((1, RNN_WIDTH), const2),
        ],
        out_specs=(tok_spec(RNN_WIDTH), head_spec, head_spec, head_spec),
        scratch_shapes=[
            pltpu.VMEM((MIX_SUB + SUBLANES, RNN_WIDTH), F32),
            pltpu.VMEM((1, RNN_WIDTH), F32),
        ],
        compiler_params=pltpu.CompilerParams(
            dimension_semantics=("arbitrary", "arbitrary"), vmem_limit_bytes=VMEM_LIMIT),
        name="mixer_in",
    )(x, g, w_in, conv_w, conv_b, w_gates, b_gates, lam, g_rnn)


def _softplus2(z2):
    return jnp.maximum(z2, 0.0) + jnp.log(1.0 + jnp.exp2(-jnp.abs(z2))) * LOG2E


def _suffix_sum(sp, tri):
    return _dot(sp.astype(BF16), tri)


def _sb_attn_kernel(q_s, k_s, v_s, o_ref, acc_s, car_s):
    heads, seq, _ = q_s.shape
    n_q = seq // ATT_TQ
    lane_sl = [slice(hh * SB_HEAD_DIM, (hh + 1) * SB_HEAD_DIM) for hh in range(heads)]

    krow = lax.broadcasted_iota(jnp.int32, (ATT_TK, ATT_TK), 0)
    kcol = lax.broadcasted_iota(jnp.int32, (ATT_TK, ATT_TK), 1)
    tri = jnp.where(krow >= kcol, 1.0, 0.0).astype(BF16)
    qrow = lax.broadcasted_iota(jnp.int32, (ATT_TK, ATT_TK), 0)
    qcol = lax.broadcasted_iota(jnp.int32, (ATT_TK, ATT_TK), 1)
    causal = qcol < qrow
    hs = range(heads)
    zero_car = jnp.zeros((ATT_TK, 1), F32)
    zero_acc = jnp.zeros((ATT_TK, SB_HEAD_DIM), F32)

    def mask_top(x):
        return jnp.concatenate([jnp.where(causal, x[0:ATT_TK], 0.0), x[ATT_TK:ATT_TQ]], axis=0)

    def q_body(qi, _):
        q0 = pl.multiple_of(qi * ATT_TQ, ATT_TQ)
        q1 = pl.multiple_of(q0 + ATT_TK, ATT_TK)
        qs = [q_s[hh, pl.ds(q0, ATT_TQ), :] for hh in hs]
        zu = [_dot_nt(qs[hh][ATT_TK:ATT_TQ], k_s[hh, pl.ds(q1, ATT_TK), :]) for hh in hs]
        spu = [jnp.where(causal, _softplus2(zu[hh]), 0.0) for hh in hs]
        cuu = [_suffix_sum(spu[hh], tri) for hh in hs]
        wu = [jnp.where(causal, jnp.exp2(zu[hh] - cuu[hh]), 0.0).astype(BF16) for hh in hs]
        pvu = [_dot(wu[hh], v_s[hh, pl.ds(q1, ATT_TK), :]) for hh in hs]
        zl = [_dot_nt(qs[hh], k_s[hh, pl.ds(q0, ATT_TK), :]) for hh in hs]
        spl = [mask_top(_softplus2(zl[hh])) for hh in hs]
        cul = [_suffix_sum(spl[hh], tri) for hh in hs]
        car_in = [jnp.concatenate([zero_car, cuu[hh][:, 0:1]], axis=0) for hh in hs]
        wl = [mask_top(jnp.exp2(zl[hh] - cul[hh] - car_in[hh])).astype(BF16) for hh in hs]
        car0 = [car_in[hh] + cul[hh][:, 0:1] for hh in hs]
        pvl = [_dot(wl[hh], v_s[hh, pl.ds(q0, ATT_TK), :]) for hh in hs]
        has_prev = qi > 0
        kp = pl.multiple_of(jnp.maximum(2 * qi - 1, 0) * ATT_TK, ATT_TK)
        zp = [_dot_nt(qs[hh][0:ATT_TK], k_s[hh, pl.ds(kp, ATT_TK), :]) for hh in hs]
        cup = [_suffix_sum(_softplus2(zp[hh]), tri) for hh in hs]
        wp = [jnp.where(has_prev, jnp.exp2(zp[hh] - cup[hh] - car0[hh][0:ATT_TK]), 0.0).astype(BF16)
              for hh in hs]
        pvp = [_dot(wp[hh], v_s[hh, pl.ds(kp, ATT_TK), :]) for hh in hs]
        car_top = [car0[hh][0:ATT_TK] + jnp.where(has_prev, cup[hh][:, 0:1], 0.0) for hh in hs]
        car_bot = [car0[hh][ATT_TK:ATT_TQ] for hh in hs]
        for hh in hs:
            acc_s[hh] = pvl[hh] + jnp.concatenate([pvp[hh], pvu[hh]], axis=0)
            car_s[hh] = jnp.concatenate([car_top[hh], car_bot[hh]], axis=0)

        def any_live(cars):
            low = cars[0]
            for hh in hs[1:]:
                low = jnp.minimum(low, cars[hh])
            return (jnp.min(low) < ATT_UNDERFLOW_LOG2).astype(jnp.int32)

        def walk(lo, blk0, cars0):
            rows = slice(lo, lo + ATT_TK)

            def cond(state):
                blk, live = state
                return jnp.logical_and(blk >= 0, live > 0)

            def body(state):
                blk, _ = state
                k0 = pl.multiple_of(blk * ATT_TK, ATT_TK)
                zs = [_dot_nt(qs[hh][rows], k_s[hh, pl.ds(k0, ATT_TK), :]) for hh in hs]
                ccs = [_suffix_sum(_softplus2(zs[hh]), tri) for hh in hs]
                cars = [car_s[hh, rows, :] for hh in hs]
                wws = [jnp.exp2(zs[hh] - ccs[hh] - cars[hh]).astype(BF16) for hh in hs]
                pvs = [_dot(wws[hh], v_s[hh, pl.ds(k0, ATT_TK), :]) for hh in hs]
                new_cars = [cars[hh] + ccs[hh][:, 0:1] for hh in hs]
                for hh in hs:
                    acc_s[hh, rows, :] += pvs[hh]
                    car_s[hh, rows, :] = new_cars[hh]
                return blk - 1, any_live(new_cars)

            lax.while_loop(cond, body, (blk0, any_live(cars0)))

        walk(0, 2 * qi - 2, car_top)
        walk(ATT_TK, 2 * qi - 1, car_bot)

        for hh in hs:
            o_ref[pl.ds(q0, ATT_TQ), lane_sl[hh]] = acc_s[hh]
        return 0

    lax.fori_loop(0, n_q, q_body, 0)


def _sb_attn(q, k, v):
    bsz, n_heads, seq, _ = q.shape
    hps = ATT_HEADS_PER_STEP
    lanes = hps * SB_HEAD_DIM
    in_spec = pl.BlockSpec((None, hps, seq, SB_HEAD_DIM), lambda b, p: (b, p, 0, 0))
    return pl.pallas_call(
        _sb_attn_kernel,
        out_shape=jax.ShapeDtypeStruct((bsz, seq, n_heads * SB_HEAD_DIM), F32),
        grid=(bsz, n_heads // hps),
        in_specs=[in_spec, in_spec, in_spec],
        out_specs=pl.BlockSpec((None, seq, lanes), lambda b, p: (b, 0, p)),
        scratch_shapes=[
            pltpu.VMEM((hps, ATT_TQ, SB_HEAD_DIM), F32),
            pltpu.VMEM((hps, ATT_TQ, 1), F32),
        ],
        compiler_params=pltpu.CompilerParams(
            dimension_semantics=("arbitrary", "arbitrary"), vmem_limit_bytes=VMEM_LIMIT),
        name="sb_attn",
    )(q, k, v)


def _mem_kv_kernel(m_ref, g_ref, wk_ref, wv_ref, k_ref, v_ref):
    hm = _rms(m_ref[...], g_ref[...]).astype(BF16)
    k_ref[...] = _dot(hm, wk_ref[...]).astype(BF16)
    v_ref[...] = _dot(hm, wv_ref[...]).astype(BF16)


def _mem_kv(mem, g, w_k, w_v):
    bsz, n_mem, _ = mem.shape
    const2 = lambda b: (0, 0)
    spec = pl.BlockSpec((None, n_mem, D_MODEL), lambda b: (b, 0, 0))
    sds = jax.ShapeDtypeStruct((bsz, n_mem, D_MODEL), BF16)
    return pl.pallas_call(
        _mem_kv_kernel,
        out_shape=(sds, sds),
        grid=(bsz,),
        in_specs=[spec, pl.BlockSpec((1, D_MODEL), const2),
                  pl.BlockSpec((D_MODEL, D_MODEL), const2), pl.BlockSpec((D_MODEL, D_MODEL), const2)],
        out_specs=(spec, spec),
        compiler_params=pltpu.CompilerParams(
            dimension_semantics=("arbitrary",), vmem_limit_bytes=VMEM_LIMIT),
        name="mem_kv",
    )(mem, g, w_k, w_v)


def _mix_xattn_kernel(x_ref, yr_ref, ysb_ref, gsb_ref, wout_ref, gxa_ref, wq_ref, km_ref, vm_ref, wo_ref,
                      o_ref):
    tile = x_ref.shape[0]
    parts = [slice(r0, r0 + XA_SUB) for r0 in range(0, tile, XA_SUB)]
    scale = XA_HEAD_DIM ** -0.5
    ys = [jnp.concatenate([yr_ref[r, :], _rms(ysb_ref[r, :], gsb_ref[...]).astype(BF16)], axis=-1)
          for r in parts]
    x1s = [x_ref[r, :] + _dot(y, wout_ref[...]) for r, y in zip(parts, ys)]
    qs = [_dot(_rms(x1, gxa_ref[...]).astype(BF16), wq_ref[...]).astype(BF16) for x1 in x1s]
    os = []
    for q in qs:
        outs = []
        for hd in range(XA_HEADS):
            sl = slice(hd * XA_HEAD_DIM, (hd + 1) * XA_HEAD_DIM)
            s = _dot_nt(q[:, sl], km_ref[:, sl]) * scale
            m = jnp.max(s, axis=-1, keepdims=True)
            p = jnp.exp(s - m)
            den = jnp.sum(p, axis=-1, keepdims=True)
            p = (p / den).astype(BF16)
            outs.append(_dot(p, vm_ref[:, sl]).astype(BF16))
        os.append(jnp.concatenate(outs, axis=-1))
    for r, x1, o in zip(parts, x1s, os):
        o_ref[r, :] = x1 + _dot(o, wo_ref[...])


def _mix_xattn(x, yr, ysb, g_sb, w_out, g_xa, w_q, kmem, vmem, w_o):
    bsz, seq, _ = x.shape
    n_mem = kmem.shape[1]
    tile = XA_TILE
    const2 = lambda b, t: (0, 0)
    tok_spec = lambda width: pl.BlockSpec((None, tile, width), lambda b, t: (b, t, 0))
    mem_spec = pl.BlockSpec((None, n_mem, D_MODEL), lambda b, t: (b, 0, 0))
    w_spec = pl.BlockSpec((D_MODEL, D_MODEL), const2)
    return pl.pallas_call(
        _mix_xattn_kernel,
        out_shape=jax.ShapeDtypeStruct((bsz, seq, D_MODEL), F32),
        grid=(bsz, seq // tile),
        in_specs=[tok_spec(D_MODEL), tok_spec(RNN_WIDTH), tok_spec(SB_WIDTH),
                  pl.BlockSpec((1, SB_WIDTH), const2), w_spec,
                  pl.BlockSpec((1, D_MODEL), const2), w_spec, mem_spec, mem_spec, w_spec],
        out_specs=tok_spec(D_MODEL),
        compiler_params=pltpu.CompilerParams(
            dimension_semantics=("arbitrary", "arbitrary"), vmem_limit_bytes=VMEM_LIMIT),
        name="mix_xattn",
    )(x, yr, ysb, g_sb, w_out, g_xa, w_q, kmem, vmem, w_o)


N_ROUTE = N_GROUPS + N_GROUPS * EXPERTS_PER_GROUP
ROUTE_PAD = 128


def _select_by(idx, rows):
    out = rows[-1]
    for i in range(len(rows) - 2, -1, -1):
        out = jnp.where(idx == i, rows[i], out)
    return out


def _first_argmax_rows(rows):
    m = rows[0]
    for r in rows[1:]:
        m = jnp.maximum(m, r)
    idx = jnp.full(m.shape, len(rows) - 1, jnp.int32)
    for i in range(len(rows) - 2, -1, -1):
        idx = jnp.where(rows[i] == m, i, idx)
    return m, idx


def _moe_final_kernel(xc_ref, xn_ref, gmoe_ref, wr_ref, br_ref, wg_ref, wu_ref, wd_ref, gfin_ref,
                      o_ref, acc_s, u_s, hb_s, sel_s, selc_s, cwh_s, cwl_s, cnt_s):
    tile = xc_ref.shape[0]
    cap = MOE_CAP
    ng, ne = N_GROUPS, EXPERTS_PER_GROUP
    groups = range(ng)

    @pl.when(pl.program_id(0) == 0)
    def _():
        r = lax.broadcasted_iota(jnp.int32, (tile, tile), 0)
        c = lax.broadcasted_iota(jnp.int32, (tile, tile), 1)
        u_s[...] = jnp.where(r < c, 1.0, 0.0).astype(BF16)
        hb_s[...] = jnp.zeros_like(hb_s)
        sel_s[...] = jnp.full(sel_s.shape, -1.0, F32)
        selc_s[...] = jnp.full(selc_s.shape, -1.0, F32)
        cwh_s[...] = jnp.zeros_like(cwh_s)
        cwl_s[...] = jnp.zeros_like(cwl_s)
        for g in groups:
            cnt_s[g] = jnp.int32(0)

    def route_logits(xt):
        hb = _rms(xt, gmoe_ref[...]).astype(BF16)
        return hb, _dot(hb, wr_ref[...]) + br_ref[...]

    def route_choice(logits):
        lt = logits.T
        row = lambda i: lt[i:i + 1, :]
        gmax, gidx = _first_argmax_rows([row(g) for g in groups])
        gden = jnp.exp(row(0) - gmax)
        for g in range(1, ng):
            gden = gden + jnp.exp(row(g) - gmax)
        group_p = 1.0 / gden
        el = [_select_by(gidx, [row(ng + g * ne + e) for g in groups]) for e in range(ne)]
        m1, i1 = _first_argmax_rows(el)
        m2, i2 = _first_argmax_rows([jnp.where(i1 == e, -jnp.inf, el[e]) for e in range(ne)])
        e2 = jnp.exp(m2 - m1)
        w1 = group_p / (1.0 + e2)
        w2 = w1 * e2
        comb = [jnp.where(i1 == e, w1, 0.0) + jnp.where(i2 == e, w2, 0.0) for e in range(ne)]
        rid = lax.broadcasted_iota(jnp.int32, (2 * SUBLANES, tile), 0)
        onehot = jnp.where(rid == gidx, 1.0, 0.0)
        before = _dot(onehot.astype(BF16), u_s[...])
        return gidx, comb, before

    def route_layout(choice):
        gidx, comb, before = choice
        rid = lax.broadcasted_iota(jnp.int32, (2 * SUBLANES, tile), 0)
        pos = _select_by(gidx, [before[g:g + 1, :] for g in groups])
        cw = jnp.zeros((2 * SUBLANES, tile), F32)
        for e in range(ne):
            cw = jnp.where(rid == e, comb[e], cw)
        cw_hi = cw.astype(BF16)
        cw_lo = (cw - cw_hi.astype(F32)).astype(BF16)
        rid_wide = lax.broadcasted_iota(jnp.int32, (ROUTE_PAD, tile), 0)
        sel_rows = jnp.where(rid_wide == gidx, pos, -1.0)
        sel_cols = sel_rows.T
        counts = [jnp.sum(jnp.where(gidx == g, 1, 0)) for g in groups]
        return sel_rows, sel_cols, cw_hi, cw_lo, counts

    slot_r = lax.broadcasted_iota(jnp.int32, (cap, tile), 0).astype(F32)
    slot_c = lax.broadcasted_iota(jnp.int32, (tile, cap), 1).astype(F32)

    def gather_rows(sel_r, base):
        gather = jnp.where(sel_r == slot_r + base, 1.0, 0.0).astype(BF16)
        xc = _dot(gather, hb_s[...]).astype(BF16)
        wc = _dot_nt(gather, cwh_s[...]) + _dot_nt(gather, cwl_s[...])
        return xc, wc

    def experts_up(xcs, gs):
        pairs = [(i, e) for i in range(len(xcs)) for e in range(ne)]
        hgs = [_dot(xcs[i], wg_ref[gs[i] * ne + e]) for i, e in pairs]
        hus = [_dot(xcs[i], wu_ref[gs[i] * ne + e]) for i, e in pairs]
        return pairs, hgs, hus

    def experts_down(up, wcs, gs):
        pairs, hgs, hus = up
        hids = [((hgs[n] * jax.nn.sigmoid(hgs[n])) * hus[n] * wcs[i][:, e:e + 1]).astype(BF16)
                for n, (i, e) in enumerate(pairs)]
        return [_dot(jnp.concatenate(hids[i * ne:(i + 1) * ne], axis=-1), wd_ref[gs[i]]).astype(BF16)
                for i in range(len(wcs))]

    def scatter_rows(sel_c, base, y):
        scatter = jnp.where(sel_c == slot_c + base, 1.0, 0.0).astype(BF16)
        return _dot(scatter, y)

    acc_s[...] = jnp.zeros_like(acc_s)

    def overflow_group(g, _):
        count = cnt_s[g]
        extra = jnp.int32(0)
        for c in range(1, -(-tile // cap)):
            extra = extra + (count > c * cap).astype(jnp.int32)
        sel_r = sel_s[pl.ds(g, 1), :]
        sel_c = selc_s[g][:, 0:1]

        def chunk(c, _):
            base = (c * cap).astype(F32)
            xc, wc = gather_rows(sel_r, base)
            acc_s[...] += scatter_rows(sel_c, base, experts_down(experts_up([xc], [g]), [wc], [g])[0])
            return 0

        lax.fori_loop(1, 1 + extra, chunk, 0)
        return 0

    lax.fori_loop(0, ng, overflow_group, 0)

    rows_in = [gather_rows(sel_s[g:g + 1, :], 0.0) for g in groups]
    xcs = [r[0] for r in rows_in]
    ups = [experts_up(xcs[0:1], [0])]
    nxt_hb, nxt_logits = route_logits(xn_ref[...])
    ups.append(experts_up(xcs[1:2], [1]))
    nxt_choice = route_choice(nxt_logits)
    ups.append(experts_up(xcs[2:3], [2]))
    nxt_sel_rows, nxt_sel_cols, nxt_cw_hi, nxt_cw_lo, nxt_counts = route_layout(nxt_choice)
    ups.append(experts_up(xcs[3:4], [3]))
    up = ([(g, e) for g in groups for e in range(ne)],
          [h for u in ups for h in u[1]], [h for u in ups for h in u[2]])
    anchor = nxt_sel_cols[0:1, 0:1] * 0.0

    ys = experts_down(up, [r[1] + anchor for r in rows_in], list(groups))
    moe = scatter_rows(selc_s[0][:, 0:1], 0.0, ys[0])
    for g in groups[1:]:
        moe = moe + scatter_rows(selc_s[g][:, 0:1], 0.0, ys[g])
    o_ref[...] = _rms(xc_ref[...] + acc_s[...] + moe, gfin_ref[...])

    hb_s[...] = nxt_hb
    sel_s[...] = nxt_sel_rows
    cwh_s[...] = nxt_cw_hi
    cwl_s[...] = nxt_cw_lo
    for g in groups:
        selc_s[g] = jnp.broadcast_to(nxt_sel_cols[:, g:g + 1], (tile, ROUTE_PAD))
        cnt_s[g] = nxt_counts[g]


def _moe_final(x, g_moe, w_router, b_router, w_gate, w_up, w_down, g_final):
    n_tok = x.shape[0]
    tile = MOE_TILE
    n_exp = w_gate.shape[0]
    const2 = lambda i: (0, 0)
    const3 = lambda i: (0, 0, 0)
    resident = pl.Buffered(1)
    n_tiles = n_tok // tile
    cur_spec = pl.BlockSpec((tile, D_MODEL), lambda i: (jnp.maximum(i - 1, 0), 0))
    nxt_spec = pl.BlockSpec((tile, D_MODEL), lambda i: (jnp.minimum(i, n_tiles - 1), 0))
    return pl.pallas_call(
        _moe_final_kernel,
        out_shape=jax.ShapeDtypeStruct((n_tok, D_MODEL), F32),
        grid=(n_tiles + 1,),
        in_specs=[
            cur_spec,
            nxt_spec,
            pl.BlockSpec((1, D_MODEL), const2),
            pl.BlockSpec((D_MODEL, ROUTE_PAD), const2),
            pl.BlockSpec((1, ROUTE_PAD), const2),
            pl.BlockSpec((n_exp, D_MODEL, D_EXPERT), const3, pipeline_mode=resident),
            pl.BlockSpec((n_exp, D_MODEL, D_EXPERT), const3, pipeline_mode=resident),
            pl.BlockSpec(w_down.shape, const3, pipeline_mode=resident),
            pl.BlockSpec((1, D_MODEL), const2),
        ],
        out_specs=cur_spec,
        scratch_shapes=[
            pltpu.VMEM((tile, D_MODEL), F32),
            pltpu.VMEM((tile, tile), BF16),
            pltpu.VMEM((tile, D_MODEL), BF16),
            pltpu.VMEM((ROUTE_PAD, tile), F32),
            pltpu.VMEM((N_GROUPS, tile, ROUTE_PAD), F32),
            pltpu.VMEM((2 * SUBLANES, tile), BF16),
            pltpu.VMEM((2 * SUBLANES, tile), BF16),
            pltpu.SMEM((N_GROUPS,), jnp.int32),
        ],
        compiler_params=pltpu.CompilerParams(
            dimension_semantics=("arbitrary",), vmem_limit_bytes=VMEM_LIMIT),
        name="moe_final",
    )(x, x, g_moe, w_router, b_router, w_gate, w_up, w_down, g_final)


def _block_diag_gates(w_a, w_x):
    half_heads = RNN_HEADS // 2
    half = RNN_WIDTH // 2
    out = jnp.zeros((2, half, 2 * half), F32)
    for c in range(2):
        for hh in range(half_heads):
            r0 = hh * RNN_HEAD_DIM
            out = out.at[c, r0:r0 + RNN_HEAD_DIM, r0:r0 + RNN_HEAD_DIM].set(w_a[c * half_heads + hh])
            out = out.at[c, r0:r0 + RNN_HEAD_DIM, half + r0:half + r0 + RNN_HEAD_DIM].set(w_x[c * half_heads + hh])
    return out.astype(BF16)


def _router_weights(w_group, b_group, w_expert, b_expert):
    w = jnp.concatenate(
        [w_group, jnp.transpose(w_expert, (1, 0, 2)).reshape(D_MODEL, N_GROUPS * EXPERTS_PER_GROUP)], axis=-1)
    w = jnp.pad(w, ((0, 0), (0, ROUTE_PAD - N_ROUTE)))
    b = jnp.concatenate([b_group, b_expert.reshape(-1)])
    b = jnp.pad(b, (0, ROUTE_PAD - N_ROUTE)).reshape(1, ROUTE_PAD)
    return w.astype(BF16), b


def _layer(x, mem, p):
    bsz, seq, _ = x.shape
    row = lambda a: a.reshape(1, -1)
    w_gates = _block_diag_gates(p["lru_w_a"], p["lru_w_x"])
    b_gates = jnp.stack([p["lru_b_a"].reshape(-1), p["lru_b_x"].reshape(-1)])
    yr, q, k, v = _mixer_in(x, row(p["norm_mix"]), p["w_in"].astype(BF16), p["conv_w"], row(p["conv_b"]),
                            w_gates, b_gates, row(p["lru_lambda"]), row(p["norm_rnn_out"]))
    ysb = _sb_attn(q, k, v)
    kmem, vmem = _mem_kv(mem, row(p["norm_mem"]), p["xa_w_k"].astype(BF16), p["xa_w_v"].astype(BF16))
    x2 = _mix_xattn(x, yr, ysb, row(p["norm_sb_out"]), p["w_out"].astype(BF16), row(p["norm_xattn"]),
                    p["xa_w_q"].astype(BF16), kmem, vmem, p["xa_w_o"].astype(BF16))
    w_router, b_router = _router_weights(p["w_group_router"], p["b_group_router"],
                                         p["w_expert_router"], p["b_expert_router"])
    ne = N_GROUPS * EXPERTS_PER_GROUP
    out = _moe_final(x2.reshape(bsz * seq, D_MODEL), row(p["norm_moe"]), w_router, b_router,
                     p["w_gate"].reshape(ne, D_MODEL, D_EXPERT).astype(BF16),
                     p["w_up"].reshape(ne, D_MODEL, D_EXPERT).astype(BF16),
                     p["w_down"].reshape(N_GROUPS, EXPERTS_PER_GROUP * D_EXPERT, D_MODEL).astype(BF16),
                     row(p["norm_final"]))
    return out.reshape(bsz, seq, D_MODEL)


def kernel(x, mem, norm_mix, w_in, conv_w, conv_b, lru_w_a, lru_b_a, lru_w_x, lru_b_x, lru_lambda, norm_rnn_out, norm_sb_out, w_out, norm_xattn, norm_mem, xa_w_q, xa_w_k, xa_w_v, xa_w_o, norm_moe, w_group_router, b_group_router, w_expert_router, b_expert_router, w_gate, w_up, w_down, norm_final):
    depth = norm_mix.shape[0]
    assert depth == 1, "the fused final RMSNorm assumes a single layer"
    l = 0
    params = dict(
        norm_mix=norm_mix[l], w_in=w_in[l], conv_w=conv_w[l], conv_b=conv_b[l],
        lru_w_a=lru_w_a[l], lru_b_a=lru_b_a[l], lru_w_x=lru_w_x[l], lru_b_x=lru_b_x[l],
        lru_lambda=lru_lambda[l], norm_rnn_out=norm_rnn_out[l], norm_sb_out=norm_sb_out[l],
        w_out=w_out[l], norm_xattn=norm_xattn[l], norm_mem=norm_mem[l],
        xa_w_q=xa_w_q[l], xa_w_k=xa_w_k[l], xa_w_v=xa_w_v[l], xa_w_o=xa_w_o[l],
        norm_moe=norm_moe[l], w_group_router=w_group_router[l], b_group_router=b_group_router[l],
        w_expert_router=w_expert_router[l], b_expert_router=b_expert_router[l],
        w_gate=w_gate[l], w_up=w_up[l], w_down=w_down[l], norm_final=norm_final)
    return _layer(x, mem, params)
```

```python
import functools
import math

import jax
import jax.numpy as jnp
from jax import lax
from jax.experimental import pallas as pl
from jax.experimental.pallas import tpu as pltpu

F32 = jnp.float32
BF16 = jnp.bfloat16

D_MODEL = 1024
RNN_WIDTH = 512
RNN_HEADS = 8
RNN_HEAD_DIM = 64
CONV_WIDTH = 4
LRU_C = 8.0
SB_WIDTH = 512
SB_HEADS = 8
SB_HEAD_DIM = 64
PROJ_WIDTH = 2 * RNN_WIDTH + 3 * SB_WIDTH
XA_HEADS = 4
XA_HEAD_DIM = D_MODEL // XA_HEADS
N_GROUPS = 4
EXPERTS_PER_GROUP = 4
D_EXPERT = D_MODEL // 4
EPS = 1e-6

SUBLANES = 8
WEIGHT_CAST_COLS = 256
VMEM_LIMIT = 56 * 1024 * 1024

LOG2E = 1.4426950408889634
Q_SCALE = (SB_HEAD_DIM ** -0.5) * LOG2E

MIX_TILE = 1024
MIX_SUB = 128
ATT_TK = 256
ATT_TQ = 2 * ATT_TK
ATT_HEADS_PER_STEP = 4
ATT_UNDERFLOW_LOG2 = 160.0
XA_TILE = 1024
XA_SUB = 512
MOE_TILE = 512
MOE_CAP = 160


def _rms(x, gain):
    var = jnp.mean(x * x, axis=-1, keepdims=True)
    return x * lax.rsqrt(var + EPS) * gain


def _dot(a, b):
    return jnp.dot(a, b, preferred_element_type=F32)


def _dot_nt(a, b):
    return lax.dot_general(a, b, (((1,), (1,)), ((), ())), preferred_element_type=F32)


def _cast_weight_once(first_step, w_ref, w_s):
    @pl.when(first_step)
    def _():
        cols = w_ref.shape[1]
        for c0 in range(0, cols, WEIGHT_CAST_COLS):
            w_s[:, c0:c0 + WEIGHT_CAST_COLS] = w_ref[:, c0:c0 + WEIGHT_CAST_COLS].astype(BF16)


def _mixer_in_kernel(x_ref, g_ref, w_ref, cw_ref, cb_ref, wg_ref, bg_ref, lam_ref, grnn_ref,
                     yr_ref, q_ref, k_ref, v_ref,
                     xpad, hcar, w_s):
    t = pl.program_id(1)
    tile = x_ref.shape[0]
    sub = MIX_SUB
    half = RNN_WIDTH // 2
    _cast_weight_once(jnp.logical_and(pl.program_id(0) == 0, t == 0), w_ref, w_s)

    @pl.when(t == 0)
    def _():
        xpad[0:SUBLANES, :] = jnp.zeros((SUBLANES, RNN_WIDTH), F32)
        hcar[...] = jnp.zeros_like(hcar)

    lam = lam_ref[...]
    softplus_neg_lam = jnp.maximum(-lam, 0.0) + jnp.log(1.0 + jnp.exp(-jnp.abs(lam)))
    row = lax.broadcasted_iota(jnp.int32, (SUBLANES, RNN_WIDTH), 0)

    def project(s):
        rows = slice(s * sub, (s + 1) * sub)
        h = _rms(x_ref[rows, :], g_ref[...]).astype(BF16)
        pr = _dot(h, w_s[:, 0:2 * RNN_WIDTH])
        qkv = _dot(h, w_s[:, 2 * RNN_WIDTH:PROJ_WIDTH])
        for hd in range(SB_HEADS):
            c0 = hd * SB_HEAD_DIM
            q_ref[hd, rows, :] = (qkv[:, c0:c0 + SB_HEAD_DIM] * Q_SCALE).astype(BF16)
            k_ref[hd, rows, :] = qkv[:, SB_WIDTH + c0:SB_WIDTH + c0 + SB_HEAD_DIM].astype(BF16)
            v_ref[hd, rows, :] = qkv[:, 2 * SB_WIDTH + c0:2 * SB_WIDTH + c0 + SB_HEAD_DIM].astype(BF16)
        return pr

    def rnn_branch(s, pr):
        rows = slice(s * sub, (s + 1) * sub)
        xpad[SUBLANES:SUBLANES + sub, :] = pr[:, 0:RNN_WIDTH]
        xc = cb_ref[...] + cw_ref[0:1, :] * xpad[SUBLANES - 3:SUBLANES - 3 + sub, :]
        for kk in range(1, CONV_WIDTH):
            off = SUBLANES - (CONV_WIDTH - 1) + kk
            xc = xc + cw_ref[kk:kk + 1, :] * xpad[off:off + sub, :]
        xpad[0:SUBLANES, :] = xpad[sub:sub + SUBLANES, :]

        xcb = xc.astype(BF16)
        g0 = _dot(xcb[:, 0:half], wg_ref[0])
        g1 = _dot(xcb[:, half:RNN_WIDTH], wg_ref[1])
        r_pre = jnp.concatenate([g0[:, 0:half], g1[:, 0:half]], axis=-1) + bg_ref[0:1, :]
        i_pre = jnp.concatenate([g0[:, half:2 * half], g1[:, half:2 * half]], axis=-1) + bg_ref[1:2, :]
        r = jax.nn.sigmoid(r_pre)
        i = jax.nn.sigmoid(i_pre)
        a = jnp.exp((-LRU_C) * r * softplus_neg_lam)
        b = jnp.sqrt(1.0 - a * a) * (i * xc)

        hc = hcar[...]
        hseq = []
        for gi in range(sub // SUBLANES):
            ag = a[gi * SUBLANES:(gi + 1) * SUBLANES, :]
            bgr = b[gi * SUBLANES:(gi + 1) * SUBLANES, :]
            for d in (1, 2, 4):
                a_sh = jnp.where(row >= d, pltpu.roll(ag, d, 0), 1.0)
                b_sh = jnp.where(row >= d, pltpu.roll(bgr, d, 0), 0.0)
                bgr = ag * b_sh + bgr
                ag = ag * a_sh
            hg = ag * hc + bgr
            hseq.append(hg)
            hc = hg[SUBLANES - 1:SUBLANES, :]
        hcar[...] = hc

        gate = pr[:, RNN_WIDTH:2 * RNN_WIDTH]
        gelu = 0.5 * gate * (1.0 + jnp.tanh(math.sqrt(2.0 / math.pi) * (gate + 0.044715 * (gate * gate * gate))))
        y = jnp.concatenate(hseq, axis=0) * gelu
        yr_ref[rows, :] = _rms(y, grnn_ref[...]).astype(BF16)

    n_sub = tile // sub
    pending = project(0)
    for s in range(n_sub):
        nxt = project(s + 1) if s + 1 < n_sub else None
        rnn_branch(s, pending)
        pending = nxt


def _mixer_in(x, g, w_in, conv_w, conv_b, w_gates, b_gates, lam, g_rnn):
    bsz, seq, _ = x.shape
    tile = MIX_TILE
    grid = (bsz, seq // tile)
    const2 = lambda b, t: (0, 0)
    tok_spec = lambda width: pl.BlockSpec((None, tile, width), lambda b, t: (b, t, 0))
    yr_sds = jax.ShapeDtypeStruct((bsz, seq, RNN_WIDTH), BF16)
    head_sds = jax.ShapeDtypeStruct((bsz, SB_HEADS, seq, SB_HEAD_DIM), BF16)
    head_spec = pl.BlockSpec((None, SB_HEADS, tile, SB_HEAD_DIM), lambda b, t: (b, 0, t, 0))
    return pl.pallas_call(
        _mixer_in_kernel,
        out_shape=(yr_sds, head_sds, head_sds, head_sds),
        grid=grid,
        in_specs=[
            tok_spec(D_MODEL),
            pl.BlockSpec((1, D_MODEL), const2),
            pl.BlockSpec((D_MODEL, PROJ_WIDTH), const2, pipeline_mode=pl.Buffered(1)),
            pl.BlockSpec((CONV_WIDTH, RNN_WIDTH), const2),
            pl.BlockSpec((1, RNN_WIDTH), const2),
            pl.BlockSpec((2, RNN_WIDTH // 2, RNN_WIDTH), lambda b, t: (0, 0, 0)),
            pl.BlockSpec((2, RNN_WIDTH), const2),
            pl.BlockSpec((1, RNN_WIDTH), const2),
            pl.BlockSpec((1, RNN_WIDTH), const2),
        ],
        out_specs=(tok_spec(RNN_WIDTH), head_spec, head_spec, head_spec),
        scratch_shapes=[
            pltpu.VMEM((MIX_SUB + SUBLANES, RNN_WIDTH), F32),
            pltpu.VMEM((1, RNN_WIDTH), F32),
            pltpu.VMEM((D_MODEL, PROJ_WIDTH), BF16),
        ],
        compiler_params=pltpu.CompilerParams(
            dimension_semantics=("arbitrary", "arbitrary"), vmem_limit_bytes=VMEM_LIMIT),
        name="mixer_in",
    )(x, g, w_in, conv_w, conv_b, w_gates, b_gates, lam, g_rnn)


def _softplus2(z2):
    return jnp.maximum(z2, 0.0) + jnp.log(1.0 + jnp.exp2(-jnp.abs(z2))) * LOG2E


def _suffix_sum(sp, tri):
    return _dot(sp.astype(BF16), tri)


def _sb_attn_kernel(q_s, k_s, v_s, o_ref, acc_s, car_s):
    heads, seq, _ = q_s.shape
    n_q = seq // ATT_TQ
    lane_sl = [slice(hh * SB_HEAD_DIM, (hh + 1) * SB_HEAD_DIM) for hh in range(heads)]

    krow = lax.broadcasted_iota(jnp.int32, (ATT_TK, ATT_TK), 0)
    kcol = lax.broadcasted_iota(jnp.int32, (ATT_TK, ATT_TK), 1)
    tri = jnp.where(krow >= kcol, 1.0, 0.0).astype(BF16)
    qrow = lax.broadcasted_iota(jnp.int32, (ATT_TK, ATT_TK), 0)
    qcol = lax.broadcasted_iota(jnp.int32, (ATT_TK, ATT_TK), 1)
    causal = qcol < qrow
    hs = range(heads)
    zero_car = jnp.zeros((ATT_TK, 1), F32)
    zero_acc = jnp.zeros((ATT_TK, SB_HEAD_DIM), F32)

    def mask_top(x):
        return jnp.concatenate([jnp.where(causal, x[0:ATT_TK], 0.0), x[ATT_TK:ATT_TQ]], axis=0)

    def q_body(qi, _):
        q0 = pl.multiple_of(qi * ATT_TQ, ATT_TQ)
        q1 = pl.multiple_of(q0 + ATT_TK, ATT_TK)
        qs = [q_s[hh, pl.ds(q0, ATT_TQ), :] for hh in hs]
        has_prev = qi > 0
        kp = pl.multiple_of(jnp.maximum(2 * qi - 1, 0) * ATT_TK, ATT_TK)
        zu = [_dot_nt(qs[hh][ATT_TK:ATT_TQ], k_s[hh, pl.ds(q1, ATT_TK), :]) for hh in hs]
        zl = [_dot_nt(qs[hh], k_s[hh, pl.ds(q0, ATT_TK), :]) for hh in hs]
        spu = [jnp.where(causal, _softplus2(zu[hh]), 0.0) for hh in hs]
        cuu = [_suffix_sum(spu[hh], tri) for hh in hs]
        zp = [_dot_nt(qs[hh][0:ATT_TK], k_s[hh, pl.ds(kp, ATT_TK), :]) for hh in hs]
        spl = [mask_top(_softplus2(zl[hh])) for hh in hs]
        wu = [jnp.where(causal, jnp.exp2(zu[hh] - cuu[hh]), 0.0).astype(BF16) for hh in hs]
        cul = [_suffix_sum(spl[hh], tri) for hh in hs]
        spp = [_softplus2(zp[hh]) for hh in hs]
        pvu = [_dot(wu[hh], v_s[hh, pl.ds(q1, ATT_TK), :]) for hh in hs]
        cup = [_suffix_sum(spp[hh], tri) for hh in hs]
        car_in = [jnp.concatenate([zero_car, cuu[hh][:, 0:1]], axis=0) for hh in hs]
        wl = [mask_top(jnp.exp2(zl[hh] - cul[hh] - car_in[hh])).astype(BF16) for hh in hs]
        car0 = [car_in[hh] + cul[hh][:, 0:1] for hh in hs]
        pvl = [_dot(wl[hh], v_s[hh, pl.ds(q0, ATT_TK), :]) for hh in hs]
        wp = [jnp.where(has_prev, jnp.exp2(zp[hh] - cup[hh] - car0[hh][0:ATT_TK]), 0.0).astype(BF16)
              for hh in hs]
        pvp = [_dot(wp[hh], v_s[hh, pl.ds(kp, ATT_TK), :]) for hh in hs]
        car_top = [car0[hh][0:ATT_TK] + jnp.where(has_prev, cup[hh][:, 0:1], 0.0) for hh in hs]
        car_bot = [car0[hh][ATT_TK:ATT_TQ] for hh in hs]
        for hh in hs:
            acc_s[hh] = pvl[hh] + jnp.concatenate([pvp[hh], pvu[hh]], axis=0)
            car_s[hh] = jnp.concatenate([car_top[hh], car_bot[hh]], axis=0)

        def any_live(cars):
            low = cars[0]
            for hh in hs[1:]:
                low = jnp.minimum(low, cars[hh])
            return (jnp.min(low) < ATT_UNDERFLOW_LOG2).astype(jnp.int32)

        def walk(lo, blk0, cars0):
            rows = slice(lo, lo + ATT_TK)

            def cond(state):
                blk, live = state
                return jnp.logical_and(blk >= 0, live > 0)

            def body(state):
                blk, _ = state
                k0 = pl.multiple_of(blk * ATT_TK, ATT_TK)
                zs = [_dot_nt(qs[hh][rows], k_s[hh, pl.ds(k0, ATT_TK), :]) for hh in hs]
                ccs = [_suffix_sum(_softplus2(zs[hh]), tri) for hh in hs]
                cars = [car_s[hh, rows, :] for hh in hs]
                wws = [jnp.exp2(zs[hh] - ccs[hh] - cars[hh]).astype(BF16) for hh in hs]
                pvs = [_dot(wws[hh], v_s[hh, pl.ds(k0, ATT_TK), :]) for hh in hs]
                new_cars = [cars[hh] + ccs[hh][:, 0:1] for hh in hs]
                for hh in hs:
                    acc_s[hh, rows, :] += pvs[hh]
                    car_s[hh, rows, :] = new_cars[hh]
                return blk - 1, any_live(new_cars)

            lax.while_loop(cond, body, (blk0, any_live(cars0)))

        walk(0, 2 * qi - 2, car_top)
        walk(ATT_TK, 2 * qi - 1, car_bot)

        for hh in hs:
            o_ref[pl.ds(q0, ATT_TQ), lane_sl[hh]] = acc_s[hh]
        return 0

    lax.fori_loop(0, n_q, q_body, 0)


def _sb_attn(q, k, v):
    bsz, n_heads, seq, _ = q.shape
    hps = ATT_HEADS_PER_STEP
    lanes = hps * SB_HEAD_DIM
    in_spec = pl.BlockSpec((None, hps, seq, SB_HEAD_DIM), lambda b, p: (b, p, 0, 0))
    return pl.pallas_call(
        _sb_attn_kernel,
        out_shape=jax.ShapeDtypeStruct((bsz, seq, n_heads * SB_HEAD_DIM), F32),
        grid=(bsz, n_heads // hps),
        in_specs=[in_spec, in_spec, in_spec],
        out_specs=pl.BlockSpec((None, seq, lanes), lambda b, p: (b, 0, p)),
        scratch_shapes=[
            pltpu.VMEM((hps, ATT_TQ, SB_HEAD_DIM), F32),
            pltpu.VMEM((hps, ATT_TQ, 1), F32),
        ],
        compiler_params=pltpu.CompilerParams(
            dimension_semantics=("arbitrary", "arbitrary"), vmem_limit_bytes=VMEM_LIMIT),
        name="sb_attn",
    )(q, k, v)


def _mem_kv_kernel(m_ref, g_ref, wk_ref, wv_ref, k_ref, v_ref, wk_s, wv_s):
    first = pl.program_id(0) == 0
    _cast_weight_once(first, wk_ref, wk_s)
    _cast_weight_once(first, wv_ref, wv_s)
    hm = _rms(m_ref[...], g_ref[...]).astype(BF16)
    k_ref[...] = _dot(hm, wk_s[...]).astype(BF16)
    v_ref[...] = _dot(hm, wv_s[...]).astype(BF16)


def _mem_kv(mem, g, w_k, w_v):
    bsz, n_mem, _ = mem.shape
    const2 = lambda b: (0, 0)
    spec = pl.BlockSpec((None, n_mem, D_MODEL), lambda b: (b, 0, 0))
    sds = jax.ShapeDtypeStruct((bsz, n_mem, D_MODEL), BF16)
    return pl.pallas_call(
        _mem_kv_kernel,
        out_shape=(sds, sds),
        grid=(bsz,),
        in_specs=[spec, pl.BlockSpec((1, D_MODEL), const2),
                  pl.BlockSpec((D_MODEL, D_MODEL), const2, pipeline_mode=pl.Buffered(1)),
                  pl.BlockSpec((D_MODEL, D_MODEL), const2, pipeline_mode=pl.Buffered(1))],
        out_specs=(spec, spec),
        scratch_shapes=[pltpu.VMEM((D_MODEL, D_MODEL), BF16), pltpu.VMEM((D_MODEL, D_MODEL), BF16)],
        compiler_params=pltpu.CompilerParams(
            dimension_semantics=("arbitrary",), vmem_limit_bytes=VMEM_LIMIT),
        name="mem_kv",
    )(mem, g, w_k, w_v)


def _mix_xattn_kernel(x_ref, yr_ref, ysb_ref, gsb_ref, wout_ref, gxa_ref, wq_ref, km_ref, vm_ref, wo_ref,
                      o_ref, wout_s, wq_s, wo_s):
    first = jnp.logical_and(pl.program_id(0) == 0, pl.program_id(1) == 0)
    _cast_weight_once(first, wout_ref, wout_s)
    _cast_weight_once(first, wq_ref, wq_s)
    _cast_weight_once(first, wo_ref, wo_s)
    tile = x_ref.shape[0]
    parts = [slice(r0, r0 + XA_SUB) for r0 in range(0, tile, XA_SUB)]
    scale = XA_HEAD_DIM ** -0.5
    ys = [jnp.concatenate([yr_ref[r, :], _rms(ysb_ref[r, :], gsb_ref[...]).astype(BF16)], axis=-1)
          for r in parts]
    x1s = [x_ref[r, :] + _dot(y, wout_s[...]) for r, y in zip(parts, ys)]
    qs = [_dot(_rms(x1, gxa_ref[...]).astype(BF16), wq_s[...]).astype(BF16) for x1 in x1s]
    os = []
    for q in qs:
        outs = []
        for hd in range(XA_HEADS):
            sl = slice(hd * XA_HEAD_DIM, (hd + 1) * XA_HEAD_DIM)
            s = _dot_nt(q[:, sl], km_ref[:, sl]) * scale
            m = jnp.max(s, axis=-1, keepdims=True)
            p = jnp.exp(s - m)
            den = jnp.sum(p, axis=-1, keepdims=True)
            p = (p / den).astype(BF16)
            outs.append(_dot(p, vm_ref[:, sl]).astype(BF16))
        os.append(jnp.concatenate(outs, axis=-1))
    for r, x1, o in zip(parts, x1s, os):
        o_ref[r, :] = x1 + _dot(o, wo_s[...])


def _mix_xattn(x, yr, ysb, g_sb, w_out, g_xa, w_q, kmem, vmem, w_o):
    bsz, seq, _ = x.shape
    n_mem = kmem.shape[1]
    tile = XA_TILE
    const2 = lambda b, t: (0, 0)
    tok_spec = lambda width: pl.BlockSpec((None, tile, width), lambda b, t: (b, t, 0))
    mem_spec = pl.BlockSpec((None, n_mem, D_MODEL), lambda b, t: (b, 0, 0))
    w_spec = pl.BlockSpec((D_MODEL, D_MODEL), const2, pipeline_mode=pl.Buffered(1))
    w_scratch = pltpu.VMEM((D_MODEL, D_MODEL), BF16)
    return pl.pallas_call(
        _mix_xattn_kernel,
        out_shape=jax.ShapeDtypeStruct((bsz, seq, D_MODEL), F32),
        grid=(bsz, seq // tile),
        in_specs=[tok_spec(D_MODEL), tok_spec(RNN_WIDTH), tok_spec(SB_WIDTH),
                  pl.BlockSpec((1, SB_WIDTH), const2), w_spec,
                  pl.BlockSpec((1, D_MODEL), const2), w_spec, mem_spec, mem_spec, w_spec],
        out_specs=tok_spec(D_MODEL),
        scratch_shapes=[w_scratch, w_scratch, w_scratch],
        compiler_params=pltpu.CompilerParams(
            dimension_semantics=("arbitrary", "arbitrary"), vmem_limit_bytes=VMEM_LIMIT),
        name="mix_xattn",
    )(x, yr, ysb, g_sb, w_out, g_xa, w_q, kmem, vmem, w_o)


N_ROUTE = N_GROUPS + N_GROUPS * EXPERTS_PER_GROUP
ROUTE_PAD = 128


def _select_by(idx, rows):
    out = rows[-1]
    for i in range(len(rows) - 2, -1, -1):
        out = jnp.where(idx == i, rows[i], out)
    return out


def _first_argmax_rows(rows):
    m = rows[0]
    for r in rows[1:]:
        m = jnp.maximum(m, r)
    idx = jnp.full(m.shape, len(rows) - 1, jnp.int32)
    for i in range(len(rows) - 2, -1, -1):
        idx = jnp.where(rows[i] == m, i, idx)
    return m, idx


def _moe_final_kernel(xc_ref, xn_ref, gmoe_ref, wr_ref, br_ref, wg_ref, wu_ref, wd_ref, gfin_ref,
                      o_ref, acc_s, u_s, hb_s, sel_s, selc_s, cwh_s, cwl_s, cnt_s):
    tile = xc_ref.shape[0]
    cap = MOE_CAP
    ng, ne = N_GROUPS, EXPERTS_PER_GROUP
    groups = range(ng)

    @pl.when(pl.program_id(0) == 0)
    def _():
        r = lax.broadcasted_iota(jnp.int32, (tile, tile), 0)
        c = lax.broadcasted_iota(jnp.int32, (tile, tile), 1)
        u_s[...] = jnp.where(r < c, 1.0, 0.0).astype(BF16)
        hb_s[...] = jnp.zeros_like(hb_s)
        sel_s[...] = jnp.full(sel_s.shape, -1.0, F32)
        selc_s[...] = jnp.full(selc_s.shape, -1.0, F32)
        cwh_s[...] = jnp.zeros_like(cwh_s)
        cwl_s[...] = jnp.zeros_like(cwl_s)
        for g in groups:
            cnt_s[g] = jnp.int32(0)

    def route_logits(xt):
        hb = _rms(xt, gmoe_ref[...]).astype(BF16)
        return hb, _dot(hb, wr_ref[...]) + br_ref[...]

    def route_choice(logits):
        lt = logits.T
        row = lambda i: lt[i:i + 1, :]
        gmax, gidx = _first_argmax_rows([row(g) for g in groups])
        gden = jnp.exp(row(0) - gmax)
        for g in range(1, ng):
            gden = gden + jnp.exp(row(g) - gmax)
        group_p = 1.0 / gden
        el = [_select_by(gidx, [row(ng + g * ne + e) for g in groups]) for e in range(ne)]
        m1, i1 = _first_argmax_rows(el)
        m2, i2 = _first_argmax_rows([jnp.where(i1 == e, -jnp.inf, el[e]) for e in range(ne)])
        e2 = jnp.exp(m2 - m1)
        w1 = group_p / (1.0 + e2)
        w2 = w1 * e2
        comb = [jnp.where(i1 == e, w1, 0.0) + jnp.where(i2 == e, w2, 0.0) for e in range(ne)]
        rid = lax.broadcasted_iota(jnp.int32, (2 * SUBLANES, tile), 0)
        onehot = jnp.where(rid == gidx, 1.0, 0.0)
        before = _dot(onehot.astype(BF16), u_s[...])
        return gidx, comb, before

    def route_layout(choice):
        gidx, comb, before = choice
        rid = lax.broadcasted_iota(jnp.int32, (2 * SUBLANES, tile), 0)
        pos = _select_by(gidx, [before[g:g + 1, :] for g in groups])
        cw = jnp.zeros((2 * SUBLANES, tile), F32)
        for e in range(ne):
            cw = jnp.where(rid == e, comb[e], cw)
        cw_hi = cw.astype(BF16)
        cw_lo = (cw - cw_hi.astype(F32)).astype(BF16)
        rid_wide = lax.broadcasted_iota(jnp.int32, (ROUTE_PAD, tile), 0)
        sel_rows = jnp.where(rid_wide == gidx, pos, -1.0)
        sel_cols = sel_rows.T
        counts = [jnp.sum(jnp.where(gidx == g, 1, 0)) for g in groups]
        return sel_rows, sel_cols, cw_hi, cw_lo, counts

    slot_r = lax.broadcasted_iota(jnp.int32, (cap, tile), 0).astype(F32)
    slot_c = lax.broadcasted_iota(jnp.int32, (tile, cap), 1).astype(F32)

    def gather_rows(sel_r, base):
        gather = jnp.where(sel_r == slot_r + base, 1.0, 0.0).astype(BF16)
        xc = _dot(gather, hb_s[...]).astype(BF16)
        wc = _dot_nt(gather, cwh_s[...]) + _dot_nt(gather, cwl_s[...])
        return xc, wc

    def experts_up(xcs, gs):
        pairs = [(i, e) for i in range(len(xcs)) for e in range(ne)]
        hgs = [_dot(xcs[i], wg_ref[gs[i] * ne + e]) for i, e in pairs]
        hus = [_dot(xcs[i], wu_ref[gs[i] * ne + e]) for i, e in pairs]
        return pairs, hgs, hus

    def experts_down(up, wcs, gs):
        pairs, hgs, hus = up
        hids = [((hgs[n] * jax.nn.sigmoid(hgs[n])) * hus[n] * wcs[i][:, e:e + 1]).astype(BF16)
                for n, (i, e) in enumerate(pairs)]
        return [_dot(jnp.concatenate(hids[i * ne:(i + 1) * ne], axis=-1), wd_ref[gs[i]]).astype(BF16)
                for i in range(len(wcs))]

    def scatter_rows(sel_c, base, y):
        scatter = jnp.where(sel_c == slot_c + base, 1.0, 0.0).astype(BF16)
        return _dot(scatter, y)

    acc_s[...] = jnp.zeros_like(acc_s)

    def overflow_group(g, _):
        count = cnt_s[g]
        extra = jnp.int32(0)
        for c in range(1, -(-tile // cap)):
            extra = extra + (count > c * cap).astype(jnp.int32)
        sel_r = sel_s[pl.ds(g, 1), :]
        sel_c = selc_s[g][:, 0:1]

        def chunk(c, _):
            base = (c * cap).astype(F32)
            xc, wc = gather_rows(sel_r, base)
            acc_s[...] += scatter_rows(sel_c, base, experts_down(experts_up([xc], [g]), [wc], [g])[0])
            return 0

        lax.fori_loop(1, 1 + extra, chunk, 0)
        return 0

    lax.fori_loop(0, ng, overflow_group, 0)

    rows_in = [gather_rows(sel_s[g:g + 1, :], 0.0) for g in groups]
    xcs = [r[0] for r in rows_in]
    ups = [experts_up(xcs[0:1], [0])]
    nxt_hb, nxt_logits = route_logits(xn_ref[...])
    ups.append(experts_up(xcs[1:2], [1]))
    nxt_choice = route_choice(nxt_logits)
    ups.append(experts_up(xcs[2:3], [2]))
    nxt_sel_rows, nxt_sel_cols, nxt_cw_hi, nxt_cw_lo, nxt_counts = route_layout(nxt_choice)
    ups.append(experts_up(xcs[3:4], [3]))
    up = ([(g, e) for g in groups for e in range(ne)],
          [h for u in ups for h in u[1]], [h for u in ups for h in u[2]])
    anchor = nxt_sel_cols[0:1, 0:1] * 0.0

    ys = experts_down(up, [r[1] + anchor for r in rows_in], list(groups))
    moe = scatter_rows(selc_s[0][:, 0:1], 0.0, ys[0])
    for g in groups[1:]:
        moe = moe + scatter_rows(selc_s[g][:, 0:1], 0.0, ys[g])
    o_ref[...] = _rms(xc_ref[...] + acc_s[...] + moe, gfin_ref[...])

    hb_s[...] = nxt_hb
    sel_s[...] = nxt_sel_rows
    cwh_s[...] = nxt_cw_hi
    cwl_s[...] = nxt_cw_lo
    for g in groups:
        selc_s[g] = jnp.broadcast_to(nxt_sel_cols[:, g:g + 1], (tile, ROUTE_PAD))
        cnt_s[g] = nxt_counts[g]


def _moe_final(x, g_moe, w_router, b_router, w_gate, w_up, w_down, g_final):
    n_tok = x.shape[0]
    tile = MOE_TILE
    n_exp = w_gate.shape[0]
    const2 = lambda i: (0, 0)
    const3 = lambda i: (0, 0, 0)
    resident = pl.Buffered(1)
    n_tiles = n_tok // tile
    cur_spec = pl.BlockSpec((tile, D_MODEL), lambda i: (jnp.maximum(i - 1, 0), 0))
    nxt_spec = pl.BlockSpec((tile, D_MODEL), lambda i: (jnp.minimum(i, n_tiles - 1), 0))
    return pl.pallas_call(
        _moe_final_kernel,
        out_shape=jax.ShapeDtypeStruct((n_tok, D_MODEL), F32),
        grid=(n_tiles + 1,),
        in_specs=[
            cur_spec,
            nxt_spec,
            pl.BlockSpec((1, D_MODEL), const2),
            pl.BlockSpec((D_MODEL, ROUTE_PAD), const2),
            pl.BlockSpec((1, ROUTE_PAD), const2),
            pl.BlockSpec((n_exp, D_MODEL, D_EXPERT), const3, pipeline_mode=resident),
            pl.BlockSpec((n_exp, D_MODEL, D_EXPERT), const3, pipeline_mode=resident),
            pl.BlockSpec(w_down.shape, const3, pipeline_mode=resident),
            pl.BlockSpec((1, D_MODEL), const2),
        ],
        out_specs=cur_spec,
        scratch_shapes=[
            pltpu.VMEM((tile, D_MODEL), F32),
            pltpu.VMEM((tile, tile), BF16),
            pltpu.VMEM((tile, D_MODEL), BF16),
            pltpu.VMEM((ROUTE_PAD, tile), F32),
            pltpu.VMEM((N_GROUPS, tile, ROUTE_PAD), F32),
            pltpu.VMEM((2 * SUBLANES, tile), BF16),
            pltpu.VMEM((2 * SUBLANES, tile), BF16),
            pltpu.SMEM((N_GROUPS,), jnp.int32),
        ],
        compiler_params=pltpu.CompilerParams(
            dimension_semantics=("arbitrary",), vmem_limit_bytes=VMEM_LIMIT),
        name="moe_final",
    )(x, x, g_moe, w_router, b_router, w_gate, w_up, w_down, g_final)


def _block_diag_gates(w_a, w_x):
    half_heads = RNN_HEADS // 2
    half = RNN_WIDTH // 2
    out = jnp.zeros((2, half, 2 * half), F32)
    for c in range(2):
        for hh in range(half_heads):
            r0 = hh * RNN_HEAD_DIM
            out = out.at[c, r0:r0 + RNN_HEAD_DIM, r0:r0 + RNN_HEAD_DIM].set(w_a[c * half_heads + hh])
            out = out.at[c, r0:r0 + RNN_HEAD_DIM, half + r0:half + r0 + RNN_HEAD_DIM].set(w_x[c * half_heads + hh])
    return out.astype(BF16)


def _router_weights(w_group, b_group, w_expert, b_expert):
    w = jnp.concatenate(
        [w_group, jnp.transpose(w_expert, (1, 0, 2)).reshape(D_MODEL, N_GROUPS * EXPERTS_PER_GROUP)], axis=-1)
    w = jnp.pad(w, ((0, 0), (0, ROUTE_PAD - N_ROUTE)))
    b = jnp.concatenate([b_group, b_expert.reshape(-1)])
    b = jnp.pad(b, (0, ROUTE_PAD - N_ROUTE)).reshape(1, ROUTE_PAD)
    return w.astype(BF16), b


def _layer(x, mem, p):
    bsz, seq, _ = x.shape
    row = lambda a: a.reshape(1, -1)
    w_gates = _block_diag_gates(p["lru_w_a"], p["lru_w_x"])
    b_gates = jnp.stack([p["lru_b_a"].reshape(-1), p["lru_b_x"].reshape(-1)])
    yr, q, k, v = _mixer_in(x, row(p["norm_mix"]), p["w_in"], p["conv_w"], row(p["conv_b"]),
                            w_gates, b_gates, row(p["lru_lambda"]), row(p["norm_rnn_out"]))
    ysb = _sb_attn(q, k, v)
    kmem, vmem = _mem_kv(mem, row(p["norm_mem"]), p["xa_w_k"], p["xa_w_v"])
    x2 = _mix_xattn(x, yr, ysb, row(p["norm_sb_out"]), p["w_out"], row(p["norm_xattn"]),
                    p["xa_w_q"], kmem, vmem, p["xa_w_o"])
    w_router, b_router = _router_weights(p["w_group_router"], p["b_group_router"],
                                         p["w_expert_router"], p["b_expert_router"])
    ne = N_GROUPS * EXPERTS_PER_GROUP
    out = _moe_final(x2.reshape(bsz * seq, D_MODEL), row(p["norm_moe"]), w_router, b_router,
                     p["w_gate"].reshape(ne, D_MODEL, D_EXPERT).astype(BF16),
                     p["w_up"].reshape(ne, D_MODEL, D_EXPERT).astype(BF16),
                     p["w_down"].reshape(N_GROUPS, EXPERTS_PER_GROUP * D_EXPERT, D_MODEL).astype(BF16),
                     row(p["norm_final"]))
    return out.reshape(bsz, seq, D_MODEL)


def kernel(x, mem, norm_mix, w_in, conv_w, conv_b, lru_w_a, lru_b_a, lru_w_x, lru_b_x, lru_lambda, norm_rnn_out, norm_sb_out, w_out, norm_xattn, norm_mem, xa_w_q, xa_w_k, xa_w_v, xa_w_o, norm_moe, w_group_router, b_group_router, w_expert_router, b_expert_router, w_gate, w_up, w_down, norm_final):
    depth = norm_mix.shape[0]
    assert depth == 1, "the fused final RMSNorm assumes a single layer"
    l = 0
    params = dict(
        norm_mix=norm_mix[l], w_in=w_in[l], conv_w=conv_w[l], conv_b=conv_b[l],
        lru_w_a=lru_w_a[l], lru_b_a=lru_b_a[l], lru_w_x=lru_w_x[l], lru_b_x=lru_b_x[l],
        lru_lambda=lru_lambda[l], norm_rnn_out=norm_rnn_out[l], norm_sb_out=norm_sb_out[l],
        w_out=w_out[l], norm_xattn=norm_xattn[l], norm_mem=norm_mem[l],
        xa_w_q=xa_w_q[l], xa_w_k=xa_w_k[l], xa_w_v=xa_w_v[l], xa_w_o=xa_w_o[l],
        norm_moe=norm_moe[l], w_group_router=w_group_router[l], b_group_router=b_group_router[l],
        w_expert_router=w_expert_router[l], b_expert_router=b_expert_router[l],
        w_gate=w_gate[l], w_up=w_up[l], w_down=w_down[l], norm_final=norm_final)
    return _layer(x, mem, params)
```

```python
import functools
import math

import jax
import jax.numpy as jnp
from jax import lax
from jax.experimental import pallas as pl
from jax.experimental.pallas import tpu as pltpu

F32 = jnp.float32
BF16 = jnp.bfloat16

D_MODEL = 1024
RNN_WIDTH = 512
RNN_HEADS = 8
RNN_HEAD_DIM = 64
CONV_WIDTH = 4
LRU_C = 8.0
SB_WIDTH = 512
SB_HEADS = 8
SB_HEAD_DIM = 64
PROJ_WIDTH = 2 * RNN_WIDTH + 3 * SB_WIDTH
XA_HEADS = 4
XA_HEAD_DIM = D_MODEL // XA_HEADS
N_GROUPS = 4
EXPERTS_PER_GROUP = 4
D_EXPERT = D_MODEL // 4
EPS = 1e-6

SUBLANES = 8
WEIGHT_CAST_COLS = 256
VMEM_LIMIT = 56 * 1024 * 1024

LOG2E = 1.4426950408889634
Q_SCALE = (SB_HEAD_DIM ** -0.5) * LOG2E

MIX_TILE = 1024
MIX_SUB = 128
ATT_TK = 256
ATT_TQ = 2 * ATT_TK
ATT_HEADS_PER_STEP = 4
ATT_UNDERFLOW_LOG2 = 160.0
XA_TILE = 1024
XA_SUB = 512
MOE_TILE = 512
MOE_CAP = 160


def _rms(x, gain):
    var = jnp.mean(x * x, axis=-1, keepdims=True)
    return x * lax.rsqrt(var + EPS) * gain


def _dot(a, b):
    return jnp.dot(a, b, preferred_element_type=F32)


def _dot_nt(a, b):
    return lax.dot_general(a, b, (((1,), (1,)), ((), ())), preferred_element_type=F32)


def _cast_weight_once(first_step, w_ref, w_s):
    @pl.when(first_step)
    def _():
        cols = w_ref.shape[1]
        for c0 in range(0, cols, WEIGHT_CAST_COLS):
            w_s[:, c0:c0 + WEIGHT_CAST_COLS] = w_ref[:, c0:c0 + WEIGHT_CAST_COLS].astype(BF16)


def _mixer_in_kernel(x_ref, g_ref, w_ref, cw_ref, cb_ref, wg_ref, bg_ref, lam_ref, grnn_ref,
                     yr_ref, q_ref, k_ref, v_ref,
                     xpad, hcar, w_s):
    t = pl.program_id(1)
    tile = x_ref.shape[0]
    sub = MIX_SUB
    half = RNN_WIDTH // 2
    _cast_weight_once(jnp.logical_and(pl.program_id(0) == 0, t == 0), w_ref, w_s)

    @pl.when(t == 0)
    def _():
        xpad[0:SUBLANES, :] = jnp.zeros((SUBLANES, RNN_WIDTH), F32)
        hcar[...] = jnp.zeros_like(hcar)

    lam = lam_ref[...]
    softplus_neg_lam = jnp.maximum(-lam, 0.0) + jnp.log(1.0 + jnp.exp(-jnp.abs(lam)))
    row = lax.broadcasted_iota(jnp.int32, (SUBLANES, RNN_WIDTH), 0)

    def project(s):
        rows = slice(s * sub, (s + 1) * sub)
        h = _rms(x_ref[rows, :], g_ref[...]).astype(BF16)
        pr = _dot(h, w_s[:, 0:2 * RNN_WIDTH])
        qkv = _dot(h, w_s[:, 2 * RNN_WIDTH:PROJ_WIDTH])
        for hd in range(SB_HEADS):
            c0 = hd * SB_HEAD_DIM
            q_ref[hd, rows, :] = (qkv[:, c0:c0 + SB_HEAD_DIM] * Q_SCALE).astype(BF16)
            k_ref[hd, rows, :] = qkv[:, SB_WIDTH + c0:SB_WIDTH + c0 + SB_HEAD_DIM].astype(BF16)
            v_ref[hd, rows, :] = qkv[:, 2 * SB_WIDTH + c0:2 * SB_WIDTH + c0 + SB_HEAD_DIM].astype(BF16)
        return pr

    def rnn_branch(s, pr):
        rows = slice(s * sub, (s + 1) * sub)
        xpad[SUBLANES:SUBLANES + sub, :] = pr[:, 0:RNN_WIDTH]
        xc = cb_ref[...] + cw_ref[0:1, :] * xpad[SUBLANES - 3:SUBLANES - 3 + sub, :]
        for kk in range(1, CONV_WIDTH):
            off = SUBLANES - (CONV_WIDTH - 1) + kk
            xc = xc + cw_ref[kk:kk + 1, :] * xpad[off:off + sub, :]
        xpad[0:SUBLANES, :] = xpad[sub:sub + SUBLANES, :]

        xcb = xc.astype(BF16)
        g0 = _dot(xcb[:, 0:half], wg_ref[0])
        g1 = _dot(xcb[:, half:RNN_WIDTH], wg_ref[1])
        r_pre = jnp.concatenate([g0[:, 0:half], g1[:, 0:half]], axis=-1) + bg_ref[0:1, :]
        i_pre = jnp.concatenate([g0[:, half:2 * half], g1[:, half:2 * half]], axis=-1) + bg_ref[1:2, :]
        r = jax.nn.sigmoid(r_pre)
        i = jax.nn.sigmoid(i_pre)
        a = jnp.exp((-LRU_C) * r * softplus_neg_lam)
        b = jnp.sqrt(1.0 - a * a) * (i * xc)

        hc = hcar[...]
        hseq = []
        for gi in range(sub // SUBLANES):
            ag = a[gi * SUBLANES:(gi + 1) * SUBLANES, :]
            bgr = b[gi * SUBLANES:(gi + 1) * SUBLANES, :]
            for d in (1, 2, 4):
                a_sh = jnp.where(row >= d, pltpu.roll(ag, d, 0), 1.0)
                b_sh = jnp.where(row >= d, pltpu.roll(bgr, d, 0), 0.0)
                bgr = ag * b_sh + bgr
                ag = ag * a_sh
            hg = ag * hc + bgr
            hseq.append(hg)
            hc = hg[SUBLANES - 1:SUBLANES, :]
        hcar[...] = hc

        gate = pr[:, RNN_WIDTH:2 * RNN_WIDTH]
        gelu = 0.5 * gate * (1.0 + jnp.tanh(math.sqrt(2.0 / math.pi) * (gate + 0.044715 * (gate * gate * gate))))
        y = jnp.concatenate(hseq, axis=0) * gelu
        yr_ref[rows, :] = _rms(y, grnn_ref[...]).astype(BF16)

    n_sub = tile // sub
    pending = project(0)
    for s in range(n_sub):
        nxt = project(s + 1) if s + 1 < n_sub else None
        rnn_branch(s, pending)
        pending = nxt


def _mixer_in(x, g, w_in, conv_w, conv_b, w_gates, b_gates, lam, g_rnn):
    bsz, seq, _ = x.shape
    tile = MIX_TILE
    grid = (bsz, seq // tile)
    const2 = lambda b, t: (0, 0)
    tok_spec = lambda width: pl.BlockSpec((None, tile, width), lambda b, t: (b, t, 0))
    yr_sds = jax.ShapeDtypeStruct((bsz, seq, RNN_WIDTH), BF16)
    head_sds = jax.ShapeDtypeStruct((bsz, SB_HEADS, seq, SB_HEAD_DIM), BF16)
    head_spec = pl.BlockSpec((None, SB_HEADS, tile, SB_HEAD_DIM), lambda b, t: (b, 0, t, 0))
    return pl.pallas_call(
        _mixer_in_kernel,
        out_shape=(yr_sds, head_sds, head_sds, head_sds),
        grid=grid,
        in_specs=[
            tok_spec(D_MODEL),
            pl.BlockSpec((1, D_MODEL), const2),
            pl.BlockSpec((D_MODEL, PROJ_WIDTH), const2, pipeline_mode=pl.Buffered(1)),
            pl.BlockSpec((CONV_WIDTH, RNN_WIDTH), const2),
            pl.BlockSpec((1, RNN_WIDTH), const2),
            pl.BlockSpec((2, RNN_WIDTH // 2, RNN_WIDTH), lambda b, t: (0, 0, 0)),
            pl.BlockSpec((2, RNN_WIDTH), const2),
            pl.BlockSpec((1, RNN_WIDTH), const2),
            pl.BlockSpec((1, RNN_WIDTH), const2),
        ],
        out_specs=(tok_spec(RNN_WIDTH), head_spec, head_spec, head_spec),
        scratch_shapes=[
            pltpu.VMEM((MIX_SUB + SUBLANES, RNN_WIDTH), F32),
            pltpu.VMEM((1, RNN_WIDTH), F32),
            pltpu.VMEM((D_MODEL, PROJ_WIDTH), BF16),
        ],
        compiler_params=pltpu.CompilerParams(
            dimension_semantics=("arbitrary", "arbitrary"), vmem_limit_bytes=VMEM_LIMIT),
        name="mixer_in",
    )(x, g, w_in, conv_w, conv_b, w_gates, b_gates, lam, g_rnn)


def _softplus2(z2):
    return jnp.maximum(z2, 0.0) + jnp.log(1.0 + jnp.exp2(-jnp.abs(z2))) * LOG2E


def _suffix_sum(sp, tri):
    return _dot(sp.astype(BF16), tri)


def _sb_attn_kernel(q_s, k_s, v_s, o_ref, acc_s, car_s):
    heads, seq, _ = q_s.shape
    n_q = seq // ATT_TQ
    lane_sl = [slice(hh * SB_HEAD_DIM, (hh + 1) * SB_HEAD_DIM) for hh in range(heads)]

    krow = lax.broadcasted_iota(jnp.int32, (ATT_TK, ATT_TK), 0)
    kcol = lax.broadcasted_iota(jnp.int32, (ATT_TK, ATT_TK), 1)
    tri = jnp.where(krow >= kcol, 1.0, 0.0).astype(BF16)
    qrow = lax.broadcasted_iota(jnp.int32, (ATT_TK, ATT_TK), 0)
    qcol = lax.broadcasted_iota(jnp.int32, (ATT_TK, ATT_TK), 1)
    causal = qcol < qrow
    hs = range(heads)
    zero_car = jnp.zeros((ATT_TK, 1), F32)
    zero_acc = jnp.zeros((ATT_TK, SB_HEAD_DIM), F32)

    def mask_top(x):
        return jnp.concatenate([jnp.where(causal, x[0:ATT_TK], 0.0), x[ATT_TK:ATT_TQ]], axis=0)

    def q_body(qi, _):
        q0 = pl.multiple_of(qi * ATT_TQ, ATT_TQ)
        q1 = pl.multiple_of(q0 + ATT_TK, ATT_TK)
        qs = [q_s[hh, pl.ds(q0, ATT_TQ), :] for hh in hs]
        has_prev = qi > 0
        kp = pl.multiple_of(jnp.maximum(2 * qi - 1, 0) * ATT_TK, ATT_TK)
        zu = [_dot_nt(qs[hh][ATT_TK:ATT_TQ], k_s[hh, pl.ds(q1, ATT_TK), :]) for hh in hs]
        zl = [_dot_nt(qs[hh], k_s[hh, pl.ds(q0, ATT_TK), :]) for hh in hs]
        spu = [jnp.where(causal, _softplus2(zu[hh]), 0.0) for hh in hs]
        cuu = [_suffix_sum(spu[hh], tri) for hh in hs]
        zp = [_dot_nt(qs[hh][0:ATT_TK], k_s[hh, pl.ds(kp, ATT_TK), :]) for hh in hs]
        spl = [mask_top(_softplus2(zl[hh])) for hh in hs]
        wu = [jnp.where(causal, jnp.exp2(zu[hh] - cuu[hh]), 0.0).astype(BF16) for hh in hs]
        cul = [_suffix_sum(spl[hh], tri) for hh in hs]
        spp = [_softplus2(zp[hh]) for hh in hs]
        pvu = [_dot(wu[hh], v_s[hh, pl.ds(q1, ATT_TK), :]) for hh in hs]
        cup = [_suffix_sum(spp[hh], tri) for hh in hs]
        car_in = [jnp.concatenate([zero_car, cuu[hh][:, 0:1]], axis=0) for hh in hs]
        wl = [mask_top(jnp.exp2(zl[hh] - cul[hh] - car_in[hh])).astype(BF16) for hh in hs]
        car0 = [car_in[hh] + cul[hh][:, 0:1] for hh in hs]
        pvl = [_dot(wl[hh], v_s[hh, pl.ds(q0, ATT_TK), :]) for hh in hs]
        wp = [jnp.where(has_prev, jnp.exp2(zp[hh] - cup[hh] - car0[hh][0:ATT_TK]), 0.0).astype(BF16)
              for hh in hs]
        pvp = [_dot(wp[hh], v_s[hh, pl.ds(kp, ATT_TK), :]) for hh in hs]
        car_top = [car0[hh][0:ATT_TK] + jnp.where(has_prev, cup[hh][:, 0:1], 0.0) for hh in hs]
        car_bot = [car0[hh][ATT_TK:ATT_TQ] for hh in hs]
        for hh in hs:
            acc_s[hh] = pvl[hh] + jnp.concatenate([pvp[hh], pvu[hh]], axis=0)
            car_s[hh] = jnp.concatenate([car_top[hh], car_bot[hh]], axis=0)

        def any_live(cars):
            low = cars[0]
            for hh in hs[1:]:
                low = jnp.minimum(low, cars[hh])
            return (jnp.min(low) < ATT_UNDERFLOW_LOG2).astype(jnp.int32)

        def walk(lo, blk0, cars0):
            rows = slice(lo, lo + ATT_TK)

            def cond(state):
                blk, live = state
                return jnp.logical_and(blk >= 0, live > 0)

            def body(state):
                blk, _ = state
                k0 = pl.multiple_of(blk * ATT_TK, ATT_TK)
                zs = [_dot_nt(qs[hh][rows], k_s[hh, pl.ds(k0, ATT_TK), :]) for hh in hs]
                ccs = [_suffix_sum(_softplus2(zs[hh]), tri) for hh in hs]
                cars = [car_s[hh, rows, :] for hh in hs]
                wws = [jnp.exp2(zs[hh] - ccs[hh] - cars[hh]).astype(BF16) for hh in hs]
                pvs = [_dot(wws[hh], v_s[hh, pl.ds(k0, ATT_TK), :]) for hh in hs]
                new_cars = [cars[hh] + ccs[hh][:, 0:1] for hh in hs]
                for hh in hs:
                    acc_s[hh, rows, :] += pvs[hh]
                    car_s[hh, rows, :] = new_cars[hh]
                return blk - 1, any_live(new_cars)

            lax.while_loop(cond, body, (blk0, any_live(cars0)))

        walk(0, 2 * qi - 2, car_top)
        walk(ATT_TK, 2 * qi - 1, car_bot)

        for hh in hs:
            o_ref[pl.ds(q0, ATT_TQ), lane_sl[hh]] = acc_s[hh]
        return 0

    lax.fori_loop(0, n_q, q_body, 0)


def _sb_attn(q, k, v):
    bsz, n_heads, seq, _ = q.shape
    hps = ATT_HEADS_PER_STEP
    lanes = hps * SB_HEAD_DIM
    in_spec = pl.BlockSpec((None, hps, seq, SB_HEAD_DIM), lambda b, p: (b, p, 0, 0))
    return pl.pallas_call(
        _sb_attn_kernel,
        out_shape=jax.ShapeDtypeStruct((bsz, seq, n_heads * SB_HEAD_DIM), F32),
        grid=(bsz, n_heads // hps),
        in_specs=[in_spec, in_spec, in_spec],
        out_specs=pl.BlockSpec((None, seq, lanes), lambda b, p: (b, 0, p)),
        scratch_shapes=[
            pltpu.VMEM((hps, ATT_TQ, SB_HEAD_DIM), F32),
            pltpu.VMEM((hps, ATT_TQ, 1), F32),
        ],
        compiler_params=pltpu.CompilerParams(
            dimension_semantics=("arbitrary", "arbitrary"), vmem_limit_bytes=VMEM_LIMIT),
        name="sb_attn",
    )(q, k, v)


def _mem_kv_kernel(m_ref, g_ref, wk_ref, wv_ref, k_ref, v_ref, wk_s, wv_s):
    first = pl.program_id(0) == 0
    _cast_weight_once(first, wk_ref, wk_s)
    _cast_weight_once(first, wv_ref, wv_s)
    hm = _rms(m_ref[...], g_ref[...]).astype(BF16)
    k_ref[...] = _dot(hm, wk_s[...]).astype(BF16)
    v_ref[...] = _dot(hm, wv_s[...]).astype(BF16)


def _mem_kv(mem, g, w_k, w_v):
    bsz, n_mem, _ = mem.shape
    const2 = lambda b: (0, 0)
    spec = pl.BlockSpec((None, n_mem, D_MODEL), lambda b: (b, 0, 0))
    sds = jax.ShapeDtypeStruct((bsz, n_mem, D_MODEL), BF16)
    return pl.pallas_call(
        _mem_kv_kernel,
        out_shape=(sds, sds),
        grid=(bsz,),
        in_specs=[spec, pl.BlockSpec((1, D_MODEL), const2),
                  pl.BlockSpec((D_MODEL, D_MODEL), const2, pipeline_mode=pl.Buffered(1)),
                  pl.BlockSpec((D_MODEL, D_MODEL), const2, pipeline_mode=pl.Buffered(1))],
        out_specs=(spec, spec),
        scratch_shapes=[pltpu.VMEM((D_MODEL, D_MODEL), BF16), pltpu.VMEM((D_MODEL, D_MODEL), BF16)],
        compiler_params=pltpu.CompilerParams(
            dimension_semantics=("arbitrary",), vmem_limit_bytes=VMEM_LIMIT),
        name="mem_kv",
    )(mem, g, w_k, w_v)


def _mix_xattn_kernel(x_ref, yr_ref, ysb_ref, gsb_ref, wout_ref, gxa_ref, wq_ref, km_ref, vm_ref, wo_ref,
                      o_ref, wout_s, wq_s, wo_s):
    first = jnp.logical_and(pl.program_id(0) == 0, pl.program_id(1) == 0)
    _cast_weight_once(first, wout_ref, wout_s)
    _cast_weight_once(first, wq_ref, wq_s)
    _cast_weight_once(first, wo_ref, wo_s)
    tile = x_ref.shape[0]
    parts = [slice(r0, r0 + XA_SUB) for r0 in range(0, tile, XA_SUB)]
    scale = XA_HEAD_DIM ** -0.5
    ys = [jnp.concatenate([yr_ref[r, :], _rms(ysb_ref[r, :], gsb_ref[...]).astype(BF16)], axis=-1)
          for r in parts]
    x1s = [x_ref[r, :] + _dot(y, wout_s[...]) for r, y in zip(parts, ys)]
    qs = [_dot(_rms(x1, gxa_ref[...]).astype(BF16), wq_s[...]).astype(BF16) for x1 in x1s]
    os = []
    for q in qs:
        outs = []
        for hd in range(XA_HEADS):
            sl = slice(hd * XA_HEAD_DIM, (hd + 1) * XA_HEAD_DIM)
            s = _dot_nt(q[:, sl], km_ref[:, sl]) * scale
            m = jnp.max(s, axis=-1, keepdims=True)
            p = jnp.exp(s - m)
            den = jnp.sum(p, axis=-1, keepdims=True)
            p = (p / den).astype(BF16)
            outs.append(_dot(p, vm_ref[:, sl]).astype(BF16))
        os.append(jnp.concatenate(outs, axis=-1))
    for r, x1, o in zip(parts, x1s, os):
        o_ref[r, :] = x1 + _dot(o, wo_s[...])


def _mix_xattn(x, yr, ysb, g_sb, w_out, g_xa, w_q, kmem, vmem, w_o):
    bsz, seq, _ = x.shape
    n_mem = kmem.shape[1]
    tile = XA_TILE
    const2 = lambda b, t: (0, 0)
    tok_spec = lambda width: pl.BlockSpec((None, tile, width), lambda b, t: (b, t, 0))
    mem_spec = pl.BlockSpec((None, n_mem, D_MODEL), lambda b, t: (b, 0, 0))
    w_spec = pl.BlockSpec((D_MODEL, D_MODEL), const2, pipeline_mode=pl.Buffered(1))
    w_scratch = pltpu.VMEM((D_MODEL, D_MODEL), BF16)
    return pl.pallas_call(
        _mix_xattn_kernel,
        out_shape=jax.ShapeDtypeStruct((bsz, seq, D_MODEL), F32),
        grid=(bsz, seq // tile),
        in_specs=[tok_spec(D_MODEL), tok_spec(RNN_WIDTH), tok_spec(SB_WIDTH),
                  pl.BlockSpec((1, SB_WIDTH), const2), w_spec,
                  pl.BlockSpec((1, D_MODEL), const2), w_spec, mem_spec, mem_spec, w_spec],
        out_specs=tok_spec(D_MODEL),
        scratch_shapes=[w_scratch, w_scratch, w_scratch],
        compiler_params=pltpu.CompilerParams(
            dimension_semantics=("arbitrary", "arbitrary"), vmem_limit_bytes=VMEM_LIMIT),
        name="mix_xattn",
    )(x, yr, ysb, g_sb, w_out, g_xa, w_q, kmem, vmem, w_o)


N_ROUTE = N_GROUPS + N_GROUPS * EXPERTS_PER_GROUP
ROUTE_PAD = 128


def _select_by(idx, rows):
    out = rows[-1]
    for i in range(len(rows) - 2, -1, -1):
        out = jnp.where(idx == i, rows[i], out)
    return out


def _first_argmax_rows(rows):
    m = rows[0]
    for r in rows[1:]:
        m = jnp.maximum(m, r)
    idx = jnp.full(m.shape, len(rows) - 1, jnp.int32)
    for i in range(len(rows) - 2, -1, -1):
        idx = jnp.where(rows[i] == m, i, idx)
    return m, idx


def _moe_final_kernel(xc_ref, xn_ref, gmoe_ref, wr_ref, br_ref, wg_ref, wu_ref, wd_ref, gfin_ref,
                      o_ref, acc_s, u_s, hb_s, sel_s, selc_s, cw_s, cnt_s):
    tile = xc_ref.shape[0]
    cap = MOE_CAP
    ng, ne = N_GROUPS, EXPERTS_PER_GROUP
    groups = range(ng)

    @pl.when(pl.program_id(0) == 0)
    def _():
        r = lax.broadcasted_iota(jnp.int32, (tile, tile), 0)
        c = lax.broadcasted_iota(jnp.int32, (tile, tile), 1)
        u_s[...] = jnp.where(r < c, 1.0, 0.0).astype(BF16)
        hb_s[...] = jnp.zeros_like(hb_s)
        sel_s[...] = jnp.full(sel_s.shape, -1.0, F32)
        selc_s[...] = jnp.full(selc_s.shape, -1.0, F32)
        cw_s[...] = jnp.zeros_like(cw_s)
        for g in groups:
            cnt_s[g] = jnp.int32(0)

    def route_logits(xt):
        hb = _rms(xt, gmoe_ref[...]).astype(BF16)
        return hb, _dot(hb, wr_ref[...]) + br_ref[...]

    def route_choice(logits):
        lt = logits.T
        row = lambda i: lt[i:i + 1, :]
        gmax, gidx = _first_argmax_rows([row(g) for g in groups])
        gden = jnp.exp(row(0) - gmax)
        for g in range(1, ng):
            gden = gden + jnp.exp(row(g) - gmax)
        group_p = 1.0 / gden
        el = [_select_by(gidx, [row(ng + g * ne + e) for g in groups]) for e in range(ne)]
        m1, i1 = _first_argmax_rows(el)
        m2, i2 = _first_argmax_rows([jnp.where(i1 == e, -jnp.inf, el[e]) for e in range(ne)])
        e2 = jnp.exp(m2 - m1)
        w1 = group_p / (1.0 + e2)
        w2 = w1 * e2
        comb = [jnp.where(i1 == e, w1, 0.0) + jnp.where(i2 == e, w2, 0.0) for e in range(ne)]
        rid = lax.broadcasted_iota(jnp.int32, (2 * SUBLANES, tile), 0)
        onehot = jnp.where(rid == gidx, 1.0, 0.0)
        before = _dot(onehot.astype(BF16), u_s[...])
        return gidx, comb, before

    def route_layout(choice):
        gidx, comb, before = choice
        rid = lax.broadcasted_iota(jnp.int32, (2 * SUBLANES, tile), 0)
        pos = _select_by(gidx, [before[g:g + 1, :] for g in groups])
        cw = jnp.zeros((2 * SUBLANES, tile), F32)
        for e in range(ne):
            cw = jnp.where(rid == e, comb[e], cw)
        cw_hi = cw.astype(BF16)
        cw_hl = jnp.concatenate([cw_hi, (cw - cw_hi.astype(F32)).astype(BF16)], axis=0)
        rid_wide = lax.broadcasted_iota(jnp.int32, (ROUTE_PAD, tile), 0)
        sel_rows = jnp.where(rid_wide == gidx, pos, -1.0)
        first_slot = jnp.where(pos < cap, gidx.astype(F32) * cap + pos, -1.0)
        sel_rows = jnp.where(rid_wide == ng, first_slot, sel_rows)
        sel_cols = sel_rows.T
        counts = [jnp.sum(jnp.where(gidx == g, 1, 0)) for g in groups]
        return sel_rows, sel_cols, cw_hl, counts

    slot_r = lax.broadcasted_iota(jnp.int32, (cap, tile), 0).astype(F32)
    slot_c = lax.broadcasted_iota(jnp.int32, (tile, cap), 1).astype(F32)

    def gather_rows(sel_r, base):
        gather = jnp.where(sel_r == slot_r + base, 1.0, 0.0).astype(BF16)
        xc = _dot(gather, hb_s[...]).astype(BF16)
        wc_hl = _dot_nt(gather, cw_s[...])
        wc = wc_hl[:, 0:2 * SUBLANES] + wc_hl[:, 2 * SUBLANES:4 * SUBLANES]
        return xc, wc

    def experts_up(xcs, gs):
        pairs = [(i, e) for i in range(len(xcs)) for e in range(ne)]
        hgs = [_dot(xcs[i], wg_ref[gs[i] * ne + e]) for i, e in pairs]
        hus = [_dot(xcs[i], wu_ref[gs[i] * ne + e]) for i, e in pairs]
        return pairs, hgs, hus

    def experts_down(up, wcs, gs):
        pairs, hgs, hus = up
        hids = [((hgs[n] * jax.nn.sigmoid(hgs[n])) * hus[n] * wcs[i][:, e:e + 1]).astype(BF16)
                for n, (i, e) in enumerate(pairs)]
        return [_dot(jnp.concatenate(hids[i * ne:(i + 1) * ne], axis=-1), wd_ref[gs[i]]).astype(BF16)
                for i in range(len(wcs))]

    def scatter_rows(sel_c, base, y):
        scatter = jnp.where(sel_c == slot_c + base, 1.0, 0.0).astype(BF16)
        return _dot(scatter, y)

    acc_s[...] = jnp.zeros_like(acc_s)

    def overflow_group(g, _):
        count = cnt_s[g]
        extra = jnp.int32(0)
        for c in range(1, -(-tile // cap)):
            extra = extra + (count > c * cap).astype(jnp.int32)
        sel_r = sel_s[pl.ds(g, 1), :]
        sel_c = selc_s[g][:, 0:1]

        def chunk(c, _):
            base = (c * cap).astype(F32)
            xc, wc = gather_rows(sel_r, base)
            acc_s[...] += scatter_rows(sel_c, base, experts_down(experts_up([xc], [g]), [wc], [g])[0])
            return 0

        lax.fori_loop(1, 1 + extra, chunk, 0)
        return 0

    lax.fori_loop(0, ng, overflow_group, 0)

    rows_in = [gather_rows(sel_s[g:g + 1, :], 0.0) for g in groups]
    xcs = [r[0] for r in rows_in]
    ups = [experts_up(xcs[0:1], [0])]
    nxt_hb, nxt_logits = route_logits(xn_ref[...])
    ups.append(experts_up(xcs[1:2], [1]))
    nxt_choice = route_choice(nxt_logits)
    ups.append(experts_up(xcs[2:3], [2]))
    nxt_sel_rows, nxt_sel_cols, nxt_cw, nxt_counts = route_layout(nxt_choice)
    ups.append(experts_up(xcs[3:4], [3]))
    up = ([(g, e) for g in groups for e in range(ne)],
          [h for u in ups for h in u[1]], [h for u in ups for h in u[2]])
    anchor = nxt_sel_cols[0:1, 0:1] * 0.0

    ys = experts_down(up, [r[1] + anchor for r in rows_in], list(groups))
    slot_all = lax.broadcasted_iota(jnp.int32, (tile, ng * cap), 1).astype(F32)
    scatter_all = jnp.where(selc_s[ng][:, 0:1] == slot_all, 1.0, 0.0).astype(BF16)
    moe = _dot(scatter_all, jnp.concatenate(ys, axis=0))
    o_ref[...] = _rms(xc_ref[...] + acc_s[...] + moe, gfin_ref[...])

    hb_s[...] = nxt_hb
    sel_s[...] = nxt_sel_rows
    cw_s[...] = nxt_cw
    for g in groups:
        selc_s[g] = jnp.broadcast_to(nxt_sel_cols[:, g:g + 1], (tile, ROUTE_PAD))
        cnt_s[g] = nxt_counts[g]
    selc_s[ng] = jnp.broadcast_to(nxt_sel_cols[:, ng:ng + 1], (tile, ROUTE_PAD))


def _moe_final(x, g_moe, w_router, b_router, w_gate, w_up, w_down, g_final):
    n_tok = x.shape[0]
    tile = MOE_TILE
    n_exp = w_gate.shape[0]
    const2 = lambda i: (0, 0)
    const3 = lambda i: (0, 0, 0)
    resident = pl.Buffered(1)
    n_tiles = n_tok // tile
    cur_spec = pl.BlockSpec((tile, D_MODEL), lambda i: (jnp.maximum(i - 1, 0), 0))
    nxt_spec = pl.BlockSpec((tile, D_MODEL), lambda i: (jnp.minimum(i, n_tiles - 1), 0))
    return pl.pallas_call(
        _moe_final_kernel,
        out_shape=jax.ShapeDtypeStruct((n_tok, D_MODEL), F32),
        grid=(n_tiles + 1,),
        in_specs=[
            cur_spec,
            nxt_spec,
            pl.BlockSpec((1, D_MODEL), const2),
            pl.BlockSpec((D_MODEL, ROUTE_PAD), const2),
            pl.BlockSpec((1, ROUTE_PAD), const2),
            pl.BlockSpec((n_exp, D_MODEL, D_EXPERT), const3, pipeline_mode=resident),
            pl.BlockSpec((n_exp, D_MODEL, D_EXPERT), const3, pipeline_mode=resident),
            pl.BlockSpec(w_down.shape, const3, pipeline_mode=resident),
            pl.BlockSpec((1, D_MODEL), const2),
        ],
        out_specs=cur_spec,
        scratch_shapes=[
            pltpu.VMEM((tile, D_MODEL), F32),
            pltpu.VMEM((tile, tile), BF16),
            pltpu.VMEM((tile, D_MODEL), BF16),
            pltpu.VMEM((ROUTE_PAD, tile), F32),
            pltpu.VMEM((N_GROUPS + 1, tile, ROUTE_PAD), F32),
            pltpu.VMEM((4 * SUBLANES, tile), BF16),
            pltpu.SMEM((N_GROUPS,), jnp.int32),
        ],
        compiler_params=pltpu.CompilerParams(
            dimension_semantics=("arbitrary",), vmem_limit_bytes=VMEM_LIMIT),
        name="moe_final",
    )(x, x, g_moe, w_router, b_router, w_gate, w_up, w_down, g_final)


def _block_diag_gates(w_a, w_x):
    half_heads = RNN_HEADS // 2
    half = RNN_WIDTH // 2
    out = jnp.zeros((2, half, 2 * half), F32)
    for c in range(2):
        for hh in range(half_heads):
            r0 = hh * RNN_HEAD_DIM
            out = out.at[c, r0:r0 + RNN_HEAD_DIM, r0:r0 + RNN_HEAD_DIM].set(w_a[c * half_heads + hh])
            out = out.at[c, r0:r0 + RNN_HEAD_DIM, half + r0:half + r0 + RNN_HEAD_DIM].set(w_x[c * half_heads + hh])
    return out.astype(BF16)


def _router_weights(w_group, b_group, w_expert, b_expert):
    w = jnp.concatenate(
        [w_group, jnp.transpose(w_expert, (1, 0, 2)).reshape(D_MODEL, N_GROUPS * EXPERTS_PER_GROUP)], axis=-1)
    w = jnp.pad(w, ((0, 0), (0, ROUTE_PAD - N_ROUTE)))
    b = jnp.concatenate([b_group, b_expert.reshape(-1)])
    b = jnp.pad(b, (0, ROUTE_PAD - N_ROUTE)).reshape(1, ROUTE_PAD)
    return w.astype(BF16), b


def _layer(x, mem, p):
    bsz, seq, _ = x.shape
    row = lambda a: a.reshape(1, -1)
    w_gates = _block_diag_gates(p["lru_w_a"], p["lru_w_x"])
    b_gates = jnp.stack([p["lru_b_a"].reshape(-1), p["lru_b_x"].reshape(-1)])
    yr, q, k, v = _mixer_in(x, row(p["norm_mix"]), p["w_in"], p["conv_w"], row(p["conv_b"]),
                            w_gates, b_gates, row(p["lru_lambda"]), row(p["norm_rnn_out"]))
    ysb = _sb_attn(q, k, v)
    kmem, vmem = _mem_kv(mem, row(p["norm_mem"]), p["xa_w_k"], p["xa_w_v"])
    x2 = _mix_xattn(x, yr, ysb, row(p["norm_sb_out"]), p["w_out"], row(p["norm_xattn"]),
                    p["xa_w_q"], kmem, vmem, p["xa_w_o"])
    w_router, b_router = _router_weights(p["w_group_router"], p["b_group_router"],
                                         p["w_expert_router"], p["b_expert_router"])
    ne = N_GROUPS * EXPERTS_PER_GROUP
    out = _moe_final(x2.reshape(bsz * seq, D_MODEL), row(p["norm_moe"]), w_router, b_router,
                     p["w_gate"].reshape(ne, D_MODEL, D_EXPERT).astype(BF16),
                     p["w_up"].reshape(ne, D_MODEL, D_EXPERT).astype(BF16),
                     p["w_down"].reshape(N_GROUPS, EXPERTS_PER_GROUP * D_EXPERT, D_MODEL).astype(BF16),
                     row(p["norm_final"]))
    return out.reshape(bsz, seq, D_MODEL)


def kernel(x, mem, norm_mix, w_in, conv_w, conv_b, lru_w_a, lru_b_a, lru_w_x, lru_b_x, lru_lambda, norm_rnn_out, norm_sb_out, w_out, norm_xattn, norm_mem, xa_w_q, xa_w_k, xa_w_v, xa_w_o, norm_moe, w_group_router, b_group_router, w_expert_router, b_expert_router, w_gate, w_up, w_down, norm_final):
    depth = norm_mix.shape[0]
    assert depth == 1, "the fused final RMSNorm assumes a single layer"
    l = 0
    params = dict(
        norm_mix=norm_mix[l], w_in=w_in[l], conv_w=conv_w[l], conv_b=conv_b[l],
        lru_w_a=lru_w_a[l], lru_b_a=lru_b_a[l], lru_w_x=lru_w_x[l], lru_b_x=lru_b_x[l],
        lru_lambda=lru_lambda[l], norm_rnn_out=norm_rnn_out[l], norm_sb_out=norm_sb_out[l],
        w_out=w_out[l], norm_xattn=norm_xattn[l], norm_mem=norm_mem[l],
        xa_w_q=xa_w_q[l], xa_w_k=xa_w_k[l], xa_w_v=xa_w_v[l], xa_w_o=xa_w_o[l],
        norm_moe=norm_moe[l], w_group_router=w_group_router[l], b_group_router=b_group_router[l],
        w_expert_router=w_expert_router[l], b_expert_router=b_expert_router[l],
        w_gate=w_gate[l], w_up=w_up[l], w_down=w_down[l], norm_final=norm_final)
    return _layer(x, mem, params)
```
